```python
import math
import jax, jax.numpy as jnp
from jax import lax
import numpy as np

D_MODEL = 1024
BATCH = 8
SEQ = 2048
DEPTH = 1
DEC_BATCH = 128
DEC_SEQ = 4
PAST_LEN = 2048
PAGE_SIZE = 128

HEAD_DIM = 64
N_HEADS_SB = 8
N_HEADS_MOBA = 8
SB_WIDTH = N_HEADS_SB * HEAD_DIM
MOBA_WIDTH = N_HEADS_MOBA * HEAD_DIM
IN_WIDTH = 3 * SB_WIDTH + 3 * MOBA_WIDTH + 2 * D_MODEL
ROPE_DIM = HEAD_DIM // 4
ROPE_THETA = 500000.0
SB_Q_BLOCK = 128
MOBA_BLOCK = 256
MOBA_TOPK = 3
MOBA_Q_CHUNK = 16
N_GROUPS = 4
EXPERTS_PER_GROUP = 4
N_EXPERTS = N_GROUPS * EXPERTS_PER_GROUP
TOP_K_IN_GROUP = 2
D_EXPERT = 256
NORM_EPS = 1e-6
POOL_NUM = 5
POOL_DEN = 4

kernel_name = "stickbreak_moba_hiermoe_decode_step"


def rms_norm(x, g):
    xf = x.astype(jnp.float32)
    r = lax.rsqrt(jnp.mean(xf * xf, axis=-1, keepdims=True) + NORM_EPS)
    return (xf * r).astype(x.dtype) * g


def partial_rope(x, pos):
    half = ROPE_DIM // 2
    inv = jnp.float32(ROPE_THETA) ** (-jnp.arange(half, dtype=jnp.float32) / half)
    ang = pos.astype(jnp.float32)[:, None] * inv[None, :]
    cos = jnp.cos(ang)[None, :, None, :]
    sin = jnp.sin(ang)[None, :, None, :]
    xr = x[..., :ROPE_DIM].astype(jnp.float32)
    x1, x2 = xr[..., :half], xr[..., half:]
    rot = jnp.concatenate([x1 * cos - x2 * sin, x1 * sin + x2 * cos], axis=-1).astype(x.dtype)
    return jnp.concatenate([rot, x[..., ROPE_DIM:]], axis=-1)


def mixer_inputs(x, pos, g, w_in):
    B, T, _ = x.shape
    h = jnp.einsum('btd,de->bte', rms_norm(x, g), w_in)
    widths = [SB_WIDTH] * 3 + [MOBA_WIDTH] * 3 + [D_MODEL, D_MODEL]
    cuts = [int(c) for c in np.cumsum(widths)[:-1]]
    q_sb, k_sb, v_sb, q_m, k_m, v_m, gate_sb, gate_m = jnp.split(h, cuts, axis=-1)
    q_sb = q_sb.reshape(B, T, N_HEADS_SB, HEAD_DIM)
    k_sb = k_sb.reshape(B, T, N_HEADS_SB, HEAD_DIM)
    v_sb = v_sb.reshape(B, T, N_HEADS_SB, HEAD_DIM)
    q_m = partial_rope(q_m.reshape(B, T, N_HEADS_MOBA, HEAD_DIM), pos)
    k_m = partial_rope(k_m.reshape(B, T, N_HEADS_MOBA, HEAD_DIM), pos)
    v_m = v_m.reshape(B, T, N_HEADS_MOBA, HEAD_DIM)
    return q_sb, k_sb, v_sb, q_m, k_m, v_m, gate_sb, gate_m


def stick_breaking(q, k, v, q_pos, k_pos):
    z = jnp.einsum('bqhd,bkhd->bhqk', q, k, preferred_element_type=jnp.float32) * (HEAD_DIM ** -0.5)
    mask = (k_pos[None, :] < q_pos[:, None])[None, None]
    log_keep = jnp.where(mask, jax.nn.log_sigmoid(-z), 0.0)
    suffix = lax.cumsum(log_keep, axis=3, reverse=True) - log_keep
    a = jnp.where(mask, jnp.exp(jax.nn.log_sigmoid(z) + suffix), 0.0)
    return jnp.einsum('bhqk,bkhd->bqhd', a.astype(v.dtype), v)


def sb_prompt(q, k, v):
    B, S, H, D = q.shape
    nb = S // SB_Q_BLOCK
    qb = q.reshape(B, nb, SB_Q_BLOCK, H, D).transpose(1, 0, 2, 3, 4)
    k_pos = jnp.arange(S)

    def blk(args):
        qi, i = args
        return stick_breaking(qi, k, v, i * SB_Q_BLOCK + jnp.arange(SB_Q_BLOCK), k_pos)

    out = lax.map(blk, (qb, jnp.arange(nb)))
    return out.transpose(1, 0, 2, 3, 4).reshape(B, S, H, D)


def moba_blocks(k, v):
    B, T, H, D = k.shape
    nb = -(-T // MOBA_BLOCK)
    pad = nb * MOBA_BLOCK - T
    kp = jnp.pad(k, ((0, 0), (0, pad), (0, 0), (0, 0)))
    vp = jnp.pad(v, ((0, 0), (0, pad), (0, 0), (0, 0)))
    kb = kp.reshape(B, nb, MOBA_BLOCK, H, D).transpose(0, 3, 1, 2, 4)
    vb = vp.reshape(B, nb, MOBA_BLOCK, H, D).transpose(0, 3, 1, 2, 4)
    kmean = jnp.mean(kb.astype(jnp.float32), axis=3)
    return kb, vb, kmean


def moba_attend(q, q_pos, kb, vb, kmean):
    B, C, H, D = q.shape
    nb = kb.shape[2]
    k_sel = min(MOBA_TOPK, nb)
    own = q_pos // MOBA_BLOCK
    s_blk = jnp.einsum('bchd,bhnd->bchn', q.astype(jnp.float32), kmean)
    past = jnp.arange(nb)[None, :] < own[:, None]
    s_blk = jnp.where(past[None, :, None, :], s_blk, -jnp.inf)
    top_s, top_i = lax.top_k(s_blk, k_sel)
    idx = jnp.concatenate([top_i, jnp.broadcast_to(own[None, :, None, None], (B, C, H, 1))], axis=-1)
    valid = jnp.concatenate([jnp.isfinite(top_s), jnp.ones((B, C, H, 1), dtype=bool)], axis=-1)
    bi = jnp.arange(B)[:, None, None, None]
    hi = jnp.arange(H)[None, None, :, None]
    kg = kb[bi, hi, idx]
    vg = vb[bi, hi, idx]
    key_pos = idx[..., None] * MOBA_BLOCK + jnp.arange(MOBA_BLOCK)
    mask = valid[..., None] & (key_pos <= q_pos[None, :, None, None, None])
    logits = jnp.einsum('bchd,bchkjd->bchkj', q, kg, preferred_element_type=jnp.float32) * (HEAD_DIM ** -0.5)
    logits = jnp.where(mask, logits, -jnp.inf)
    p = jax.nn.softmax(logits.reshape(B, C, H, -1), axis=-1).reshape(logits.shape)
    return jnp.einsum('bchkj,bchkjd->bchd', p.astype(vg.dtype), vg)


def moba_sweep(q, pos0, kb, vb, kmean, chunk):
    B, T, H, D = q.shape
    n = T // chunk
    qc = q.reshape(B, n, chunk, H, D).transpose(1, 0, 2, 3, 4)

    def f(args):
        qi, i = args
        return moba_attend(qi, pos0 + i * chunk + jnp.arange(chunk), kb, vb, kmean)

    out = lax.map(f, (qc, jnp.arange(n)))
    return out.transpose(1, 0, 2, 3, 4).reshape(B, T, H, D)


def merge_branches(x, o_sb, o_m, gate_sb, gate_m, w_proj_sb, w_proj_moba, w_out):
    B, T, _ = x.shape
    b_sb = jnp.einsum('bte,ed->btd', o_sb.reshape(B, T, SB_WIDTH), w_proj_sb)
    b_m = jnp.einsum('bte,ed->btd', o_m.reshape(B, T, MOBA_WIDTH), w_proj_moba)
    m = jax.nn.sigmoid(gate_sb) * b_sb + jax.nn.sigmoid(gate_m) * b_m
    return x + jnp.einsum('btd,de->bte', m, w_out)


def hier_moe(x, w_rg, b_rg, w_re, b_re, w_gate, w_up, w_down):
    B, T, _ = x.shape
    g_prob = jax.nn.softmax((jnp.einsum('btd,dg->btg', x, w_rg) + b_rg).astype(jnp.float32), axis=-1)
    g_p, g_i = lax.top_k(g_prob, 1)
    e_logit = (jnp.einsum('btd,de->bte', x, w_re) + b_re).astype(jnp.float32)
    e_logit = e_logit.reshape(B, T, N_GROUPS, EXPERTS_PER_GROUP)
    e_logit = jnp.take_along_axis(e_logit, g_i[..., None], axis=2)[:, :, 0]
    e_p, e_i = lax.top_k(jax.nn.softmax(e_logit, axis=-1), TOP_K_IN_GROUP)
    w_sel = g_p * e_p / jnp.sum(e_p, axis=-1, keepdims=True)
    ids = g_i * EXPERTS_PER_GROUP + e_i
    combine = jnp.einsum('btk,btke->bte', w_sel, jax.nn.one_hot(ids, N_EXPERTS, dtype=jnp.float32)).astype(x.dtype)
    a = jnp.einsum('btd,edf->btef', x, w_gate)
    u = jnp.einsum('btd,edf->btef', x, w_up)
    hdn = jax.nn.silu(a) * u * combine[..., None]
    return jnp.einsum('btef,efd->btd', hdn, w_down)


def setup_inputs(seed: int = 0) -> dict:
    key = jax.random.key(seed)
    ks = jax.random.split(key, 24)
    n_pages = PAST_LEN // PAGE_SIZE
    n_phys = (DEC_BATCH * n_pages * POOL_NUM) // POOL_DEN
    f32 = jnp.float32
    nrm = lambda k, shape, scale: jax.random.normal(k, shape, f32) * scale
    pool_shape = (DEPTH, n_phys, PAGE_SIZE, N_HEADS_SB, HEAD_DIM)
    pool_shape_m = (DEPTH, n_phys, PAGE_SIZE, N_HEADS_MOBA, HEAD_DIM)
    perm = jax.random.permutation(ks[6], n_phys)
    page_table = perm[:DEC_BATCH * n_pages].reshape(DEC_BATCH, n_pages).astype(jnp.int32)
    return {
        "x_prompt": nrm(ks[0], (BATCH, SEQ, D_MODEL), 1.0),
        "x_sample": nrm(ks[1], (DEC_BATCH, DEC_SEQ, D_MODEL), 1.0),
        "cache_sb_k": nrm(ks[2], pool_shape, 1.0),
        "cache_sb_v": nrm(ks[3], pool_shape, 1.0),
        "cache_moba_k": nrm(ks[4], pool_shape_m, 1.0),
        "cache_moba_v": nrm(ks[5], pool_shape_m, 1.0),
        "page_table": page_table,
        "g_mix": 1.0 + nrm(ks[7], (DEPTH, D_MODEL), 0.01),
        "w_in": nrm(ks[8], (DEPTH, D_MODEL, IN_WIDTH), D_MODEL ** -0.5),
        "w_proj_sb": nrm(ks[9], (DEPTH, SB_WIDTH, D_MODEL), SB_WIDTH ** -0.5),
        "w_proj_moba": nrm(ks[10], (DEPTH, MOBA_WIDTH, D_MODEL), MOBA_WIDTH ** -0.5),
        "w_out": nrm(ks[11], (DEPTH, D_MODEL, D_MODEL), D_MODEL ** -0.5),
        "g_ffn": 1.0 + nrm(ks[12], (DEPTH, D_MODEL), 0.01),
        "w_router_group": nrm(ks[13], (DEPTH, D_MODEL, N_GROUPS), D_MODEL ** -0.5),
        "b_router_group": nrm(ks[14], (DEPTH, N_GROUPS), 0.01),
        "w_router_expert": nrm(ks[15], (DEPTH, D_MODEL, N_EXPERTS), D_MODEL ** -0.5),
        "b_router_expert": nrm(ks[16], (DEPTH, N_EXPERTS), 0.01),
        "w_expert_gate": nrm(ks[17], (DEPTH, N_EXPERTS, D_MODEL, D_EXPERT), D_MODEL ** -0.5),
        "w_expert_up": nrm(ks[18], (DEPTH, N_EXPERTS, D_MODEL, D_EXPERT), D_MODEL ** -0.5),
        "w_expert_down": nrm(ks[19], (DEPTH, N_EXPERTS, D_EXPERT, D_MODEL), D_EXPERT ** -0.5),
        "g_final": 1.0 + nrm(ks[20], (D_MODEL,), 0.01),
    }


def reference(x_prompt, x_sample, cache_sb_k, cache_sb_v, cache_moba_k, cache_moba_v, page_table,
              g_mix, w_in, w_proj_sb, w_proj_moba, w_out, g_ffn, w_router_group, b_router_group,
              w_router_expert, b_router_expert, w_expert_gate, w_expert_up, w_expert_down, g_final):
    seq = x_prompt.shape[1]
    dec_b, dec_seq = x_sample.shape[0], x_sample.shape[1]
    past_len = page_table.shape[1] * PAGE_SIZE
    pos_p = jnp.arange(seq)
    pos_s = past_len + jnp.arange(dec_seq)
    k_pos_s = jnp.arange(past_len + dec_seq)
    hp, hs = x_prompt, x_sample
    p_sbk, p_sbv, p_mk, p_mv = [], [], [], []
    s_sbk, s_sbv, s_mk, s_mv = [], [], [], []
    for layer in range(DEPTH):
        moe_w = (w_router_group[layer], b_router_group[layer], w_router_expert[layer],
                 b_router_expert[layer], w_expert_gate[layer], w_expert_up[layer], w_expert_down[layer])
        q_sb, k_sb, v_sb, q_m, k_m, v_m, ga, gb = mixer_inputs(hp, pos_p, g_mix[layer], w_in[layer])
        o_sb = sb_prompt(q_sb, k_sb, v_sb)
        kb, vb, km = moba_blocks(k_m, v_m)
        o_m = moba_sweep(q_m, 0, kb, vb, km, MOBA_Q_CHUNK)
        hp = merge_branches(hp, o_sb, o_m, ga, gb, w_proj_sb[layer], w_proj_moba[layer], w_out[layer])
        hp = hp + hier_moe(rms_norm(hp, g_ffn[layer]), *moe_w)
        p_sbk.append(k_sb); p_sbv.append(v_sb); p_mk.append(k_m); p_mv.append(v_m)
        q_sb, k_sb, v_sb, q_m, k_m, v_m, ga, gb = mixer_inputs(hs, pos_s, g_mix[layer], w_in[layer])
        past_sbk = cache_sb_k[layer, page_table].reshape(dec_b, past_len, N_HEADS_SB, HEAD_DIM)
        past_sbv = cache_sb_v[layer, page_table].reshape(dec_b, past_len, N_HEADS_SB, HEAD_DIM)
        o_sb = stick_breaking(q_sb, jnp.concatenate([past_sbk, k_sb], axis=1),
                              jnp.concatenate([past_sbv, v_sb], axis=1), pos_s, k_pos_s)
        past_mk = cache_moba_k[layer, page_table].reshape(dec_b, past_len, N_HEADS_MOBA, HEAD_DIM)
        past_mv = cache_moba_v[layer, page_table].reshape(dec_b, past_len, N_HEADS_MOBA, HEAD_DIM)
        kb, vb, km = moba_blocks(jnp.concatenate([past_mk, k_m], axis=1), jnp.concatenate([past_mv, v_m], axis=1))
        o_m = moba_sweep(q_m, past_len, kb, vb, km, 1)
        hs = merge_branches(hs, o_sb, o_m, ga, gb, w_proj_sb[layer], w_proj_moba[layer], w_out[layer])
        hs = hs + hier_moe(rms_norm(hs, g_ffn[layer]), *moe_w)
        s_sbk.append(k_sb); s_sbv.append(v_sb); s_mk.append(k_m); s_mv.append(v_m)
    y_prompt = rms_norm(hp, g_final)
    y_sample = rms_norm(hs, g_final)
    return (y_prompt, y_sample,
            jnp.stack(p_sbk), jnp.stack(p_sbv), jnp.stack(p_mk), jnp.stack(p_mv),
            jnp.stack(s_sbk), jnp.stack(s_sbv), jnp.stack(s_mk), jnp.stack(s_mv))
```

```python
import functools

import jax
import jax.numpy as jnp
from jax import lax
from jax.experimental import pallas as pl
from jax.experimental.pallas import tpu as pltpu

D_MODEL = 1024
HEAD_DIM = 64
N_HEADS = 8
WIDTH = N_HEADS * HEAD_DIM
IN_WIDTH = 6 * WIDTH + 2 * D_MODEL
ROPE_DIM = HEAD_DIM // 4
ROPE_HALF = ROPE_DIM // 2
ROPE_THETA = 500000.0
PAGE_SIZE = 128
MOBA_BLOCK = 256
MOBA_TOPK = 3
N_GROUPS = 4
EXPERTS_PER_GROUP = 4
N_EXPERTS = N_GROUPS * EXPERTS_PER_GROUP
D_EXPERT = 256
NORM_EPS = 1e-6
Q_SCALE = HEAD_DIM ** -0.5

LANES = 128
ATT_BLOCK = 128
VMEM_LIMIT = 56 * 1024 * 1024

_NT = (((1,), (1,)), ((), ()))


def _dot(a, b):
    return jnp.dot(a, b, preferred_element_type=jnp.float32)


def _dot_nt(a, b):
    return lax.dot_general(a, b, _NT, preferred_element_type=jnp.float32)


def _split2(x):
    hi = x.astype(jnp.bfloat16)
    lo = (x - hi.astype(jnp.float32)).astype(jnp.bfloat16)
    return hi, lo


def _split3(x):
    hi = x.astype(jnp.bfloat16)
    r = x - hi.astype(jnp.float32)
    mid = r.astype(jnp.bfloat16)
    lo = (r - mid.astype(jnp.float32)).astype(jnp.bfloat16)
    return hi, mid, lo


def _log_sigmoid(z):
    return jnp.minimum(z, 0.0) - jnp.log1p(jnp.exp(-jnp.abs(z)))


def _suffix_matrix():
    r = lax.broadcasted_iota(jnp.int32, (ATT_BLOCK, 2 * ATT_BLOCK), 0)
    c = lax.broadcasted_iota(jnp.int32, (ATT_BLOCK, 2 * ATT_BLOCK), 1)
    return jnp.where((c >= ATT_BLOCK) | (r > c), 1.0, 0.0).astype(jnp.bfloat16)


def _suffix_sums(lk, umat):
    hi, mid, lo = _split3(lk)
    return _dot(hi, umat) + _dot(mid, umat) + _dot(lo, umat)


def _rms_norm(x, g):
    r = lax.rsqrt(jnp.mean(x * x, axis=-1, keepdims=True) + NORM_EPS)
    return (x * r) * g


def _rope(h, ca, cb, cc):
    outs = []
    for g in range(WIDTH // LANES):
        xg = h[:, g * LANES:(g + 1) * LANES]
        outs.append(xg * ca + pltpu.roll(xg, LANES - ROPE_HALF, 1) * cb
                    + pltpu.roll(xg, ROPE_HALF, 1) * cc)
    return jnp.concatenate(outs, axis=1)


def _inproj_kernel(x_ref, g_ref, w_ref, ca_ref, cb_ref, cc_ref,
                   qsb_ref, ksb_ref, vsb_ref, qm_ref, km_ref, vm_ref, ga_ref, gb_ref,
                   ksb16_ref, vsb16_ref, km16_ref, vm16_ref, kmean_ref):
    xn = _rms_norm(x_ref[...], g_ref[...]).astype(jnp.bfloat16)
    ca, cb, cc = ca_ref[...], cb_ref[...], cc_ref[...]

    def proj(c, width=WIDTH):
        return _dot(xn, w_ref[:, c:c + width])

    qsb_ref[...] = (proj(0) * Q_SCALE).astype(jnp.bfloat16)
    k = proj(WIDTH)
    ksb_ref[...] = k
    ksb16_ref[...] = k.astype(jnp.bfloat16)
    v = proj(2 * WIDTH)
    vsb_ref[...] = v
    vsb16_ref[...] = v.astype(jnp.bfloat16)
    qm_ref[...] = _rope(proj(3 * WIDTH), ca, cb, cc)
    k = _rope(proj(4 * WIDTH), ca, cb, cc)
    km_ref[...] = k
    km16_ref[...] = k.astype(jnp.bfloat16)
    kmean_ref[0] = jnp.sum(k, axis=0, keepdims=True) * (1.0 / MOBA_BLOCK)
    v = proj(5 * WIDTH)
    vm_ref[...] = v
    vm16_ref[...] = v.astype(jnp.bfloat16)
    for half in range(2):
        ga_ref[:, half * WIDTH:(half + 1) * WIDTH] = proj(6 * WIDTH + half * WIDTH)
        gb_ref[:, half * WIDTH:(half + 1) * WIDTH] = proj(6 * WIDTH + D_MODEL + half * WIDTH)


def _rope_tables(pos):
    inv = jnp.float32(ROPE_THETA) ** (-jnp.arange(ROPE_HALF, dtype=jnp.float32) / ROPE_HALF)
    ang = pos.astype(jnp.float32)[:, None] * inv[None, :]
    cos, sin = jnp.cos(ang), jnp.sin(ang)
    n = pos.shape[0]
    rest = HEAD_DIM - ROPE_DIM
    one = jnp.ones((n, rest), jnp.float32)
    zero = jnp.zeros((n, rest), jnp.float32)
    zh = jnp.zeros((n, ROPE_HALF), jnp.float32)
    a = jnp.concatenate([cos, cos, one], axis=1)
    b = jnp.concatenate([-sin, zh, zero], axis=1)
    c = jnp.concatenate([zh, sin, zero], axis=1)
    rep = LANES // HEAD_DIM
    return tuple(jnp.tile(t, (1, rep)) for t in (a, b, c))


def _inproj(x, g, w16, tables, tm):
    t = x.shape[0]
    nt = tables[0].shape[0] // tm
    row = lambda i: (i, 0)
    fixed = lambda i: (0, 0)
    f32, bf16 = jnp.float32, jnp.bfloat16
    slab = lambda dt, w=WIDTH: jax.ShapeDtypeStruct((t, w), dt)
    out_shape = (slab(bf16), slab(f32), slab(f32), slab(f32), slab(f32), slab(f32),
                 slab(f32, D_MODEL), slab(f32, D_MODEL),
                 slab(bf16), slab(bf16), slab(bf16), slab(bf16),
                 jax.ShapeDtypeStruct((t // tm, 1, WIDTH), f32))
    out_specs = tuple([pl.BlockSpec((tm, WIDTH), row)] * 6
                      + [pl.BlockSpec((tm, D_MODEL), row)] * 2
                      + [pl.BlockSpec((tm, WIDTH), row)] * 4
                      + [pl.BlockSpec((1, 1, WIDTH), lambda i: (i, 0, 0))])
    tab_spec = pl.BlockSpec((tm, LANES), lambda i: (i % nt, 0))
    return pl.pallas_call(
        _inproj_kernel,
        grid=(t // tm,),
        in_specs=[pl.BlockSpec((tm, D_MODEL), row),
                  pl.BlockSpec((1, D_MODEL), fixed),
                  pl.BlockSpec((D_MODEL, IN_WIDTH), fixed),
                  tab_spec, tab_spec, tab_spec],
        out_specs=out_specs,
        out_shape=out_shape,
        compiler_params=pltpu.CompilerParams(
            dimension_semantics=("arbitrary",), vmem_limit_bytes=VMEM_LIMIT),
        name="inproj",
    )(x, g, w16, *tables)


def _sb_prompt_kernel(q_ref, k_ref, v_ref, o_ref):
    qi = pl.program_id(1)
    umat = _suffix_matrix()
    r = lax.broadcasted_iota(jnp.int32, (ATT_BLOCK, ATT_BLOCK), 0)
    c = lax.broadcasted_iota(jnp.int32, (ATT_BLOCK, ATT_BLOCK), 1)
    causal = c < r
    for h in range(N_HEADS):
        hs = slice(h * HEAD_DIM, (h + 1) * HEAD_DIM)
        qh = q_ref[:, hs]

        def tile(kb, carry, masked):
            run, acc = carry
            start = pl.multiple_of(kb * ATT_BLOCK, ATT_BLOCK)
            kh = k_ref[pl.ds(start, ATT_BLOCK), hs]
            vh = v_ref[pl.ds(start, ATT_BLOCK), hs]
            z = _dot_nt(qh, kh)
            ls = _log_sigmoid(z)
            lk = ls - z
            if masked:
                lk = jnp.where(causal, lk, 0.0)
            s = _suffix_sums(lk, umat)
            a = jnp.exp(ls + s[:, :ATT_BLOCK] + run)
            if masked:
                a = jnp.where(causal, a, 0.0)
            acc = acc + _dot(a.astype(jnp.bfloat16), vh)
            return run + s[:, ATT_BLOCK:], acc

        carry = (jnp.zeros((ATT_BLOCK, ATT_BLOCK), jnp.float32),
                 jnp.zeros((ATT_BLOCK, HEAD_DIM), jnp.float32))
        carry = tile(qi, carry, True)
        carry = lax.fori_loop(
            1, qi + 1, lambda j, cr: tile(qi - j, cr, False), carry)
        o_ref[:, hs] = carry[1].astype(o_ref.dtype)


def _sb_prompt(q16, k16, v16, batch, seq):
    nq = seq // ATT_BLOCK
    qspec = pl.BlockSpec((ATT_BLOCK, WIDTH), lambda b, i: (b * nq + i, 0))
    kvspec = pl.BlockSpec((seq, WIDTH), lambda b, i: (b, 0))
    return pl.pallas_call(
        _sb_prompt_kernel,
        grid=(batch, nq),
        in_specs=[qspec, kvspec, kvspec],
        out_specs=qspec,
        out_shape=jax.ShapeDtypeStruct((batch * seq, WIDTH), jnp.bfloat16),
        compiler_params=pltpu.CompilerParams(
            dimension_semantics=("arbitrary", "arbitrary"), vmem_limit_bytes=VMEM_LIMIT),
        name="sb_prompt",
    )(q16, k16, v16)


def _block_scores(qf, kmean_rows):
    pad = jnp.zeros((LANES - kmean_rows.shape[0], kmean_rows.shape[1]), jnp.float32)
    km = jnp.concatenate([kmean_rows, pad], axis=0)
    qh, ql = _split2(qf)
    kh, kl = _split2(km)
    return _dot_nt(qh, kh) + _dot_nt(qh, kl) + _dot_nt(ql, kh)


def _select_bias(s, n_past, n_blocks):
    lane = lax.broadcasted_iota(jnp.int32, s.shape, 1)
    rank = jnp.zeros(s.shape, jnp.float32)
    for m in range(n_blocks):
        sm = s[:, m:m + 1]
        beats = (sm > s) | ((sm == s) & (m < lane))
        rank = rank + jnp.where(beats, 1.0, 0.0) * jnp.where(m < n_past, 1.0, 0.0)
    sel = (lane < n_past) & (rank < float(MOBA_TOPK))
    return jnp.where(sel, 0.0, -jnp.inf)


def _softmax_step(logits, pv, carry):
    m, l, acc = carry
    m_new = jnp.maximum(m, jnp.max(logits, axis=1, keepdims=True))
    alpha = jnp.exp(m - m_new)
    p = jnp.exp(logits - m_new)
    l = alpha * l + jnp.sum(p, axis=1, keepdims=True)
    acc = alpha * acc + pv(p.astype(jnp.bfloat16))
    return m_new, l, acc


def _moba_prompt_kernel(q_ref, k_ref, v_ref, kmean_ref, o_ref):
    qi = pl.program_id(1)
    per_block = MOBA_BLOCK // ATT_BLOCK
    own = qi // per_block
    r = lax.broadcasted_iota(jnp.int32, (ATT_BLOCK, MOBA_BLOCK), 0)
    c = lax.broadcasted_iota(jnp.int32, (ATT_BLOCK, MOBA_BLOCK), 1)
    own_mask = c <= r + (qi % per_block) * ATT_BLOCK
    lane = lax.broadcasted_iota(jnp.int32, (ATT_BLOCK, LANES), 1)
    kmean = kmean_ref[0]
    for h in range(N_HEADS):
        hs = slice(h * HEAD_DIM, (h + 1) * HEAD_DIM)
        qf = q_ref[:, hs]
        qb = (qf * Q_SCALE).astype(jnp.bfloat16)
        bias = _select_bias(_block_scores(qf, kmean[:, hs]), own, kmean.shape[0])

        def block(n, carry, is_own):
            start = pl.multiple_of(n * MOBA_BLOCK, MOBA_BLOCK)
            kh = k_ref[pl.ds(start, MOBA_BLOCK), hs]
            vh = v_ref[pl.ds(start, MOBA_BLOCK), hs]
            logits = _dot_nt(qb, kh)
            if is_own:
                logits = jnp.where(own_mask, logits, -jnp.inf)
            else:
                logits = logits + jnp.max(
                    jnp.where(lane == n, bias, -jnp.inf), axis=1, keepdims=True)
            return _softmax_step(logits, lambda p: _dot(p, vh), carry)

        carry = (jnp.full((ATT_BLOCK, 1), -jnp.inf, jnp.float32),
                 jnp.zeros((ATT_BLOCK, 1), jnp.float32),
                 jnp.zeros((ATT_BLOCK, HEAD_DIM), jnp.float32))
        carry = block(own, carry, True)
        m, l, acc = lax.fori_loop(0, own, lambda n, cr: block(n, cr, False), carry)
        o_ref[:, hs] = (acc / l).astype(o_ref.dtype)


def _moba_prompt(qf, k16, v16, kmean, batch, seq):
    nq = seq // ATT_BLOCK
    nb = seq // MOBA_BLOCK
    qspec = pl.BlockSpec((ATT_BLOCK, WIDTH), lambda b, i: (b * nq + i, 0))
    kvspec = pl.BlockSpec((seq, WIDTH), lambda b, i: (b, 0))
    return pl.pallas_call(
        _moba_prompt_kernel,
        grid=(batch, nq),
        in_specs=[qspec, kvspec, kvspec,
                  pl.BlockSpec((1, nb, WIDTH), lambda b, i: (b, 0, 0))],
        out_specs=qspec,
        out_shape=jax.ShapeDtypeStruct((batch * seq, WIDTH), jnp.bfloat16),
        compiler_params=pltpu.CompilerParams(
            dimension_semantics=("arbitrary", "arbitrary"), vmem_limit_bytes=VMEM_LIMIT),
        name="moba_prompt",
    )(qf, k16, v16, kmean)


def _decode_kernel(pt_ref, q_ref, knew_ref, vnew_ref, *refs, moba, n_pages, n_new):
    del pt_ref
    k_pages, v_pages, o_ref = refs[:n_pages], refs[n_pages:2 * n_pages], refs[2 * n_pages]
    rows = n_new * N_HEADS
    q = q_ref[0]
    sub = lax.broadcasted_iota(jnp.int32, (N_HEADS, WIDTH), 0)
    ln = lax.broadcasted_iota(jnp.int32, (N_HEADS, WIDTH), 1)
    head_mask = (ln // HEAD_DIM) == sub
    qbd = jnp.concatenate(
        [jnp.where(head_mask, jnp.broadcast_to(q[j:j + 1, :], (N_HEADS, WIDTH)), 0.0)
         for j in range(n_new)], axis=0)
    r = lax.broadcasted_iota(jnp.int32, (rows, PAGE_SIZE), 0)
    c = lax.broadcasted_iota(jnp.int32, (rows, PAGE_SIZE), 1)
    zpad = jnp.zeros((PAGE_SIZE - n_new, WIDTH), jnp.float32)
    k_new = jnp.concatenate([knew_ref[0], zpad], axis=0).astype(jnp.bfloat16)
    v_new = jnp.concatenate([vnew_ref[0], zpad], axis=0).astype(jnp.bfloat16)
    page16 = lambda ref: ref[0].astype(jnp.bfloat16)

    if moba:
        qb = (qbd * Q_SCALE).astype(jnp.bfloat16)
        per_block = MOBA_BLOCK // PAGE_SIZE
        n_blocks = n_pages // per_block
        lane = lax.broadcasted_iota(jnp.int32, (WIDTH, LANES), 1)
        kmean_t = jnp.zeros((WIDTH, LANES), jnp.float32)
        for n in range(n_blocks):
            col = sum(jnp.sum(k_pages[n * per_block + i][0], axis=1, keepdims=True)
                      for i in range(per_block)) * (1.0 / MOBA_BLOCK)
            kmean_t = jnp.where(lane == n, col, kmean_t)
        qh, ql = _split2(qbd)
        kh, kl = _split2(kmean_t)
        scores = _dot(qh, kh) + _dot(qh, kl) + _dot(ql, kh)
        bias = _select_bias(scores, n_blocks, n_blocks)
        carry = (jnp.full((rows, 1), -jnp.inf, jnp.float32),
                 jnp.zeros((rows, 1), jnp.float32),
                 jnp.zeros((rows, WIDTH), jnp.float32))
        logits = jnp.where(c <= r // N_HEADS, _dot_nt(qb, k_new), -jnp.inf)
        carry = _softmax_step(logits, lambda p: _dot(p, v_new), carry)
        for p in range(n_pages):
            n = p // per_block
            logits = _dot(qb, page16(k_pages[p])) + bias[:, n:n + 1]
            vt = page16(v_pages[p])
            carry = _softmax_step(logits, lambda pr, vt=vt: _dot_nt(pr, vt), carry)
        m, l, acc = carry
        acc = acc / l
    else:
        qb = qbd.astype(jnp.bfloat16)
        umat = _suffix_matrix()
        new_mask = c < r // N_HEADS

        def chunk(z, av, mask, carry):
            run, acc = carry
            ls = _log_sigmoid(z)
            lk = ls - z
            if mask is not None:
                lk = jnp.where(mask, lk, 0.0)
            s = _suffix_sums(lk, umat)
            a = jnp.exp(ls + s[:, :PAGE_SIZE] + run)
            if mask is not None:
                a = jnp.where(mask, a, 0.0)
            return run + s[:, PAGE_SIZE:], acc + av(a.astype(jnp.bfloat16))

        carry = (jnp.zeros((rows, PAGE_SIZE), jnp.float32),
                 jnp.zeros((rows, WIDTH), jnp.float32))
        carry = chunk(_dot_nt(qb, k_new), lambda a: _dot(a, v_new), new_mask, carry)
        for p in reversed(range(n_pages)):
            vt = page16(v_pages[p])
            carry = chunk(_dot(qb, page16(k_pages[p])),
                          lambda a, vt=vt: _dot_nt(a, vt), None, carry)
        acc = carry[1]

    o_ref[0] = jnp.concatenate(
        [jnp.sum(jnp.where(head_mask, acc[j * N_HEADS:(j + 1) * N_HEADS, :], 0.0),
                 axis=0, keepdims=True) for j in range(n_new)], axis=0)


def _decode(page_table, q, k_new, v_new, cache_k, cache_v, moba):
    b, n_new, _ = q.shape
    n_pages = page_table.shape[1]
    tok_spec = pl.BlockSpec((1, n_new, WIDTH), lambda i, pt: (i, 0, 0))

    def page_spec(p):
        return pl.BlockSpec((1, WIDTH, PAGE_SIZE), lambda i, pt: (pt[i, p], 0, 0))

    grid_spec = pltpu.PrefetchScalarGridSpec(
        num_scalar_prefetch=1,
        grid=(b,),
        in_specs=[tok_spec, tok_spec, tok_spec]
        + [page_spec(p) for p in range(n_pages)] * 2,
        out_specs=tok_spec,
    )
    return pl.pallas_call(
        functools.partial(_decode_kernel, moba=moba, n_pages=n_pages, n_new=n_new),
        grid_spec=grid_spec,
        out_shape=jax.ShapeDtypeStruct((b, n_new, WIDTH), jnp.float32),
        compiler_params=pltpu.CompilerParams(
            dimension_semantics=("arbitrary",), vmem_limit_bytes=VMEM_LIMIT),
        name="moba_decode" if moba else "sb_decode",
    )(page_table, q, k_new, v_new, *([cache_k] * n_pages), *([cache_v] * n_pages))


def _route(lg):
    lane = lax.broadcasted_iota(jnp.int32, lg.shape, 1)
    lane_f = lane.astype(jnp.float32)
    far = float(LANES)
    is_group = lane < N_GROUPS
    gl = jnp.where(is_group, lg, -jnp.inf)
    gmax = jnp.max(gl, axis=1, keepdims=True)
    gsum = jnp.sum(jnp.where(is_group, jnp.exp(gl - gmax), 0.0), axis=1, keepdims=True)
    g_p = 1.0 / gsum
    g_i = jnp.min(jnp.where(gl == gmax, lane_f, far), axis=1, keepdims=True)
    lo = N_GROUPS + EXPERTS_PER_GROUP * g_i
    in_group = (lane_f >= lo) & (lane_f < lo + EXPERTS_PER_GROUP)
    el = jnp.where(in_group, lg, -jnp.inf)
    m1 = jnp.max(el, axis=1, keepdims=True)
    i1 = jnp.min(jnp.where(el == m1, lane_f, far), axis=1, keepdims=True)
    el2 = jnp.where(lane_f == i1, -jnp.inf, el)
    m2 = jnp.max(el2, axis=1, keepdims=True)
    i2 = jnp.min(jnp.where(el2 == m2, lane_f, far), axis=1, keepdims=True)
    t = jnp.exp(m2 - m1)
    w1 = g_p / (1.0 + t)
    w2 = w1 * t
    return jnp.where(lane_f == i1, w1, 0.0) + jnp.where(lane_f == i2, w2, 0.0)


def _merge_kernel(x_ref, osb_ref, om_ref, ga_ref, gb_ref, wps_ref, wpm_ref, wout_ref,
                  g_ref, wrh_ref, wrl_ref, br_ref, hp_ref, xn_ref, comb_ref):
    b_sb = _dot(osb_ref[...].astype(jnp.bfloat16), wps_ref[...])
    b_m = _dot(om_ref[...].astype(jnp.bfloat16), wpm_ref[...])
    m = jax.nn.sigmoid(ga_ref[...]) * b_sb + jax.nn.sigmoid(gb_ref[...]) * b_m
    hp = x_ref[...] + _dot(m.astype(jnp.bfloat16), wout_ref[...])
    hp_ref[...] = hp
    xn = _rms_norm(hp, g_ref[...])
    xn_ref[...] = xn.astype(jnp.bfloat16)
    xh, xl = _split2(xn)
    wh, wl = wrh_ref[...], wrl_ref[...]
    lg = _dot(xh, wh) + _dot(xh, wl) + _dot(xl, wh) + br_ref[...]
    comb_ref[...] = _route(lg)


def _merge(x, o_sb, o_m, ga, gb, wps, wpm, wout, g_ffn, wr_hi, wr_lo, br, tm):
    t = x.shape[0]
    row = lambda i: (i, 0)
    fixed = lambda i: (0, 0)
    return pl.pallas_call(
        _merge_kernel,
        grid=(t // tm,),
        in_specs=[pl.BlockSpec((tm, D_MODEL), row),
                  pl.BlockSpec((tm, WIDTH), row),
                  pl.BlockSpec((tm, WIDTH), row),
                  pl.BlockSpec((tm, D_MODEL), row),
                  pl.BlockSpec((tm, D_MODEL), row),
                  pl.BlockSpec((WIDTH, D_MODEL), fixed),
                  pl.BlockSpec((WIDTH, D_MODEL), fixed),
                  pl.BlockSpec((D_MODEL, D_MODEL), fixed),
                  pl.BlockSpec((1, D_MODEL), fixed),
                  pl.BlockSpec((D_MODEL, LANES), fixed),
                  pl.BlockSpec((D_MODEL, LANES), fixed),
                  pl.BlockSpec((1, LANES), fixed)],
        out_specs=(pl.BlockSpec((tm, D_MODEL), row),
                   pl.BlockSpec((tm, D_MODEL), row),
                   pl.BlockSpec((tm, LANES), row)),
        out_shape=(jax.ShapeDtypeStruct((t, D_MODEL), jnp.float32),
                   jax.ShapeDtypeStruct((t, D_MODEL), jnp.bfloat16),
                   jax.ShapeDtypeStruct((t, LANES), jnp.float32)),
        compiler_params=pltpu.CompilerParams(
            dimension_semantics=("arbitrary",), vmem_limit_bytes=VMEM_LIMIT),
        name="merge",
    )(x, o_sb, o_m, ga, gb, wps, wpm, wout, g_ffn, wr_hi, wr_lo, br)


def _moe_kernel(xn_ref, comb_ref, hp_ref, wg_ref, wu_ref, wd_ref, gf_ref, y_ref, acc_ref):
    e = pl.program_id(1)
    xn = xn_ref[...]
    a = _dot(xn, wg_ref[0])
    u = _dot(xn, wu_ref[0])
    comb = comb_ref[...]
    lane = lax.broadcasted_iota(jnp.int32, comb.shape, 1)
    w = jnp.sum(jnp.where(lane == e + N_GROUPS, comb, 0.0), axis=1, keepdims=True)
    hdn = (a * jax.nn.sigmoid(a)) * u * w
    part = _dot(hdn.astype(jnp.bfloat16), wd_ref[0])

    @pl.when(e == 0)
    def _():
        acc_ref[...] = part

    @pl.when(e > 0)
    def _():
        acc_ref[...] += part

    @pl.when(e == N_EXPERTS - 1)
    def _():
        y_ref[...] = _rms_norm(hp_ref[...] + acc_ref[...], gf_ref[...])


def _moe(xn16, comb, hp, wg, wu, wd, g_final, tm):
    t = xn16.shape[0]
    row = lambda i, e: (i, 0)
    return pl.pallas_call(
        _moe_kernel,
        grid=(t // tm, N_EXPERTS),
        in_specs=[pl.BlockSpec((tm, D_MODEL), row),
                  pl.BlockSpec((tm, LANES), row),
                  pl.BlockSpec((tm, D_MODEL), row),
                  pl.BlockSpec((1, D_MODEL, D_EXPERT), lambda i, e: (e, 0, 0)),
                  pl.BlockSpec((1, D_MODEL, D_EXPERT), lambda i, e: (e, 0, 0)),
                  pl.BlockSpec((1, D_EXPERT, D_MODEL), lambda i, e: (e, 0, 0)),
                  pl.BlockSpec((1, D_MODEL), lambda i, e: (0, 0))],
        out_specs=pl.BlockSpec((tm, D_MODEL), row),
        out_shape=jax.ShapeDtypeStruct((t, D_MODEL), jnp.float32),
        scratch_shapes=[pltpu.VMEM((tm, D_MODEL), jnp.float32)],
        compiler_params=pltpu.CompilerParams(
            dimension_semantics=("arbitrary", "arbitrary"), vmem_limit_bytes=VMEM_LIMIT),
        name="moe",
    )(xn16, comb, hp, wg, wu, wd, g_final)


def _tail(x, o_sb, o_m, ga, gb, lw, g_out, tm_merge, tm_moe):
    hp, xn16, comb = _merge(x, o_sb, o_m, ga, gb, lw["wps"], lw["wpm"], lw["wout"],
                            lw["g_ffn"], lw["wr_hi"], lw["wr_lo"], lw["br"], tm_merge)
    return _moe(xn16, comb, hp, lw["wg"], lw["wu"], lw["wd"], g_out, tm_moe)


def kernel(x_prompt, x_sample, cache_sb_k, cache_sb_v, cache_moba_k, cache_moba_v, page_table,
           g_mix, w_in, w_proj_sb, w_proj_moba, w_out, g_ffn, w_router_group, b_router_group,
           w_router_expert, b_router_expert, w_expert_gate, w_expert_up, w_expert_down, g_final):
    batch, seq, _ = x_prompt.shape
    dec_b, dec_seq, _ = x_sample.shape
    depth = w_in.shape[0]
    assert depth == 1, "the final RMSNorm is fused into the last layer's MoE kernel"
    n_phys = cache_sb_k.shape[1]
    past_len = page_table.shape[1] * PAGE_SIZE
    bf16 = jnp.bfloat16
    tm_p = 256
    tm_s = min(256, dec_b * dec_seq)

    tab_p = _rope_tables(jnp.arange(tm_p * (seq // tm_p)))
    tab_s = _rope_tables(past_len + (jnp.arange(tm_s) % dec_seq))

    hp = x_prompt.reshape(batch * seq, D_MODEL)
    hs = x_sample.reshape(dec_b * dec_seq, D_MODEL)
    outs = [[] for _ in range(8)]
    for layer in range(depth):
        w_r = jnp.concatenate([w_router_group[layer], w_router_expert[layer]], axis=1)
        w_r = jnp.pad(w_r, ((0, 0), (0, LANES - w_r.shape[1])))
        wr_hi = w_r.astype(bf16)
        b_r = jnp.concatenate([b_router_group[layer], b_router_expert[layer]])
        lw = dict(
            wps=w_proj_sb[layer].astype(bf16), wpm=w_proj_moba[layer].astype(bf16),
            wout=w_out[layer].astype(bf16), g_ffn=g_ffn[layer][None, :],
            wr_hi=wr_hi, wr_lo=(w_r - wr_hi.astype(jnp.float32)).astype(bf16),
            br=jnp.pad(b_r, (0, LANES - b_r.shape[0]))[None, :],
            wg=w_expert_gate[layer].astype(bf16), wu=w_expert_up[layer].astype(bf16),
            wd=w_expert_down[layer].astype(bf16))
        w16 = w_in[layer].astype(bf16)
        g = g_mix[layer][None, :]
        g_out = g_final[None, :]

        (q_sb, k_sb, v_sb, q_m, k_m, v_m, ga, gb,
         k_sb16, v_sb16, k_m16, v_m16, kmean) = _inproj(hp, g, w16, tab_p, tm_p)
        o_sb = _sb_prompt(q_sb, k_sb16, v_sb16, batch, seq)
        o_m = _moba_prompt(q_m, k_m16, v_m16,
                           kmean.reshape(batch, seq // MOBA_BLOCK, WIDTH), batch, seq)
        hp = _tail(hp, o_sb, o_m, ga, gb, lw, g_out, 256, 1024)
        for dst, val in zip(outs[:4], (k_sb, v_sb, k_m, v_m)):
            dst.append(val.reshape(batch, seq, N_HEADS, HEAD_DIM))

        (q_sb, k_sb, v_sb, q_m, k_m, v_m, ga, gb, *_) = _inproj(hs, g, w16, tab_s, tm_s)
        tok = lambda a: a.astype(jnp.float32).reshape(dec_b, dec_seq, WIDTH)
        pages = lambda cch: jnp.transpose(cch[layer], (0, 2, 3, 1)).reshape(
            n_phys, WIDTH, PAGE_SIZE)
        o_sb = _decode(page_table, tok(q_sb), tok(k_sb), tok(v_sb),
                       pages(cache_sb_k), pages(cache_sb_v), moba=False)
        o_m = _decode(page_table, tok(q_m), tok(k_m), tok(v_m),
                      pages(cache_moba_k), pages(cache_moba_v), moba=True)
        flat = lambda a: a.reshape(dec_b * dec_seq, WIDTH)
        hs = _tail(hs, flat(o_sb), flat(o_m), ga, gb, lw, g_out, tm_s, tm_s * 2)
        for dst, val in zip(outs[4:], (k_sb, v_sb, k_m, v_m)):
            dst.append(val.reshape(dec_b, dec_seq, N_HEADS, HEAD_DIM))

    y_prompt = hp.reshape(batch, seq, D_MODEL)
    y_sample = hs.reshape(dec_b, dec_seq, D_MODEL)
    return (y_prompt, y_sample) + tuple(jnp.stack(o) for o in outs)
```

```python
import functools

import jax
import jax.numpy as jnp
from jax import lax
from jax.experimental import pallas as pl
from jax.experimental.pallas import tpu as pltpu

D_MODEL = 1024
HEAD_DIM = 64
N_HEADS = 8
WIDTH = N_HEADS * HEAD_DIM
IN_WIDTH = 6 * WIDTH + 2 * D_MODEL
ROPE_DIM = HEAD_DIM // 4
ROPE_HALF = ROPE_DIM // 2
ROPE_THETA = 500000.0
PAGE_SIZE = 128
MOBA_BLOCK = 256
MOBA_TOPK = 3
N_GROUPS = 4
EXPERTS_PER_GROUP = 4
N_EXPERTS = N_GROUPS * EXPERTS_PER_GROUP
D_EXPERT = 256
NORM_EPS = 1e-6
Q_SCALE = HEAD_DIM ** -0.5

LANES = 128
HEAD_PAIR = LANES // HEAD_DIM
N_PAIRS = N_HEADS // HEAD_PAIR
ATT_BLOCK = 128
Q_TILE = 256
VMEM_LIMIT = 56 * 1024 * 1024

MASKED = -1e30

_NT = (((1,), (1,)), ((), ()))


def _dot(a, b):
    return jnp.dot(a, b, preferred_element_type=jnp.float32)


def _dot_nt(a, b):
    return lax.dot_general(a, b, _NT, preferred_element_type=jnp.float32)


def _split2(x):
    hi = x.astype(jnp.bfloat16)
    lo = (x - hi.astype(jnp.float32)).astype(jnp.bfloat16)
    return hi, lo


def _split3(x):
    hi = x.astype(jnp.bfloat16)
    r = x - hi.astype(jnp.float32)
    mid = r.astype(jnp.bfloat16)
    lo = (r - mid.astype(jnp.float32)).astype(jnp.bfloat16)
    return hi, mid, lo


def _log_sigmoid(z):
    return jnp.minimum(z, 0.0) - jnp.log(1.0 + jnp.exp(-jnp.abs(z)))


def _suffix_matrix():
    r = lax.broadcasted_iota(jnp.int32, (3 * ATT_BLOCK, 2 * ATT_BLOCK), 0) % ATT_BLOCK
    c = lax.broadcasted_iota(jnp.int32, (3 * ATT_BLOCK, 2 * ATT_BLOCK), 1)
    return jnp.where((c >= ATT_BLOCK) | (r > c), 1.0, 0.0).astype(jnp.bfloat16)


def _suffix_sums(lk, umat):
    return _dot(jnp.concatenate(_split3(lk), axis=1), umat)


def _rms_norm(x, g):
    r = lax.rsqrt(jnp.mean(x * x, axis=-1, keepdims=True) + NORM_EPS)
    return (x * r) * g


def _rope(h, ca, cb, cc):
    outs = []
    for g in range(WIDTH // LANES):
        xg = h[:, g * LANES:(g + 1) * LANES]
        outs.append(xg * ca + pltpu.roll(xg, LANES - ROPE_HALF, 1) * cb
                    + pltpu.roll(xg, ROPE_HALF, 1) * cc)
    return jnp.concatenate(outs, axis=1)


def _inproj_kernel(x_ref, g_ref, w_ref, ca_ref, cb_ref, cc_ref,
                   qsb_ref, ksb_ref, vsb_ref, qm_ref, km_ref, vm_ref, ga_ref, gb_ref,
                   ksb16_ref, vsb16_ref, km16_ref, vm16_ref, kmean_ref):
    xn = _rms_norm(x_ref[...], g_ref[...]).astype(jnp.bfloat16)
    ca, cb, cc = ca_ref[...], cb_ref[...], cc_ref[...]

    def proj(c, width=WIDTH):
        return _dot(xn, w_ref[:, c:c + width])

    qsb_ref[...] = (proj(0) * Q_SCALE).astype(jnp.bfloat16)
    k = proj(WIDTH)
    ksb_ref[...] = k
    ksb16_ref[...] = k.astype(jnp.bfloat16)
    v = proj(2 * WIDTH)
    vsb_ref[...] = v
    vsb16_ref[...] = v.astype(jnp.bfloat16)
    qm_ref[...] = _rope(proj(3 * WIDTH), ca, cb, cc)
    k = _rope(proj(4 * WIDTH), ca, cb, cc)
    km_ref[...] = k
    km16_ref[...] = k.astype(jnp.bfloat16)
    kmean_ref[0] = jnp.sum(k, axis=0, keepdims=True) * (1.0 / MOBA_BLOCK)
    v = proj(5 * WIDTH)
    vm_ref[...] = v
    vm16_ref[...] = v.astype(jnp.bfloat16)
    for half in range(2):
        ga_ref[:, half * WIDTH:(half + 1) * WIDTH] = proj(6 * WIDTH + half * WIDTH)
        gb_ref[:, half * WIDTH:(half + 1) * WIDTH] = proj(6 * WIDTH + D_MODEL + half * WIDTH)


def _rope_tables(pos):
    inv = jnp.float32(ROPE_THETA) ** (-jnp.arange(ROPE_HALF, dtype=jnp.float32) / ROPE_HALF)
    ang = pos.astype(jnp.float32)[:, None] * inv[None, :]
    cos, sin = jnp.cos(ang), jnp.sin(ang)
    n = pos.shape[0]
    rest = HEAD_DIM - ROPE_DIM
    one = jnp.ones((n, rest), jnp.float32)
    zero = jnp.zeros((n, rest), jnp.float32)
    zh = jnp.zeros((n, ROPE_HALF), jnp.float32)
    a = jnp.concatenate([cos, cos, one], axis=1)
    b = jnp.concatenate([-sin, zh, zero], axis=1)
    c = jnp.concatenate([zh, sin, zero], axis=1)
    rep = LANES // HEAD_DIM
    return tuple(jnp.tile(t, (1, rep)) for t in (a, b, c))


def _inproj(x, g, w16, tables, tm):
    t = x.shape[0]
    nt = tables[0].shape[0] // tm
    row = lambda i: (i, 0)
    fixed = lambda i: (0, 0)
    f32, bf16 = jnp.float32, jnp.bfloat16
    slab = lambda dt, w=WIDTH: jax.ShapeDtypeStruct((t, w), dt)
    out_shape = (slab(bf16), slab(f32), slab(f32), slab(f32), slab(f32), slab(f32),
                 slab(f32, D_MODEL), slab(f32, D_MODEL),
                 slab(bf16), slab(bf16), slab(bf16), slab(bf16),
                 jax.ShapeDtypeStruct((t // tm, 1, WIDTH), f32))
    out_specs = tuple([pl.BlockSpec((tm, WIDTH), row)] * 6
                      + [pl.BlockSpec((tm, D_MODEL), row)] * 2
                      + [pl.BlockSpec((tm, WIDTH), row)] * 4
                      + [pl.BlockSpec((1, 1, WIDTH), lambda i: (i, 0, 0))])
    tab_spec = pl.BlockSpec((tm, LANES), lambda i: (i % nt, 0))
    return pl.pallas_call(
        _inproj_kernel,
        grid=(t // tm,),
        in_specs=[pl.BlockSpec((tm, D_MODEL), row),
                  pl.BlockSpec((1, D_MODEL), fixed),
                  pl.BlockSpec((D_MODEL, IN_WIDTH), fixed),
                  tab_spec, tab_spec, tab_spec],
        out_specs=out_specs,
        out_shape=out_shape,
        compiler_params=pltpu.CompilerParams(
            dimension_semantics=("arbitrary",), vmem_limit_bytes=VMEM_LIMIT),
        name="inproj",
    )(x, g, w16, *tables)


def _stack_pair(x, zero):
    low = lax.broadcasted_iota(jnp.int32, x.shape, 1) < HEAD_DIM
    return jnp.concatenate([jnp.where(low, x, zero), jnp.where(low, zero, x)], axis=0)


def _unstack_pair(y):
    half = y.shape[0] // 2
    low = lax.broadcasted_iota(jnp.int32, (half, LANES), 1) < HEAD_DIM
    return jnp.where(low, y[:half], y[half:])


def _sb_prompt_kernel(q_ref, k_ref, v_ref, o_ref, q2_ref, run_ref, acc_ref):
    qi = pl.program_id(1)
    umat = _suffix_matrix()
    rows = HEAD_PAIR * Q_TILE
    r = lax.broadcasted_iota(jnp.int32, (rows, ATT_BLOCK), 0) % Q_TILE
    c = lax.broadcasted_iota(jnp.int32, (rows, ATT_BLOCK), 1)
    pairs = range(N_PAIRS)
    lanes = [slice(p * LANES, (p + 1) * LANES) for p in pairs]
    for p in pairs:
        qp = q_ref[:, lanes[p]]
        q2_ref[p] = _stack_pair(qp, jnp.zeros_like(qp))

    def tile(kb, first, masked):
        start = pl.multiple_of(kb * ATT_BLOCK, ATT_BLOCK)
        if masked:
            causal = kb * ATT_BLOCK + c < qi * Q_TILE + r
        z = [_dot_nt(q2_ref[p], k_ref[pl.ds(start, ATT_BLOCK), lanes[p]]) for p in pairs]
        ls = [_log_sigmoid(z[p]) for p in pairs]
        lk = [ls[p] - z[p] for p in pairs]
        if masked:
            lk = [jnp.where(causal, lk[p], 0.0) for p in pairs]
        s = [_suffix_sums(lk[p], umat) for p in pairs]
        e = [ls[p] + s[p][:, :ATT_BLOCK] for p in pairs]
        if not first:
            e = [e[p] + run_ref[p] for p in pairs]
        a = [jnp.exp(e[p]) for p in pairs]
        if masked:
            a = [jnp.where(causal, a[p], 0.0) for p in pairs]
        pv = [_unstack_pair(_dot(a[p].astype(jnp.bfloat16),
                                 v_ref[pl.ds(start, ATT_BLOCK), lanes[p]])) for p in pairs]
        for p in pairs:
            if first:
                run_ref[p] = s[p][:, ATT_BLOCK:]
                acc_ref[p] = pv[p]
            else:
                run_ref[p] += s[p][:, ATT_BLOCK:]
                acc_ref[p] += pv[p]

    last = (qi + 1) * (Q_TILE // ATT_BLOCK) - 1
    tile(last, True, True)
    for j in range(1, Q_TILE // ATT_BLOCK):
        tile(last - j, False, True)

    def body(j, carry):
        tile(last - j, False, False)
        return carry

    lax.fori_loop(Q_TILE // ATT_BLOCK, last + 1, body, 0)
    for p in pairs:
        o_ref[:, lanes[p]] = acc_ref[p].astype(o_ref.dtype)


def _sb_prompt(q16, k16, v16, batch, seq):
    nq = seq // Q_TILE
    qspec = pl.BlockSpec((Q_TILE, WIDTH), lambda b, i: (b * nq + i, 0))
    kvspec = pl.BlockSpec((seq, WIDTH), lambda b, i: (b, 0))
    stacked = (N_PAIRS, HEAD_PAIR * Q_TILE, LANES)
    return pl.pallas_call(
        _sb_prompt_kernel,
        grid=(batch, nq),
        in_specs=[qspec, kvspec, kvspec],
        out_specs=qspec,
        out_shape=jax.ShapeDtypeStruct((batch * seq, WIDTH), jnp.bfloat16),
        scratch_shapes=[pltpu.VMEM(stacked, jnp.bfloat16),
                        pltpu.VMEM(stacked, jnp.float32),
                        pltpu.VMEM((N_PAIRS, Q_TILE, LANES), jnp.float32)],
        compiler_params=pltpu.CompilerParams(
            dimension_semantics=("arbitrary", "arbitrary"), vmem_limit_bytes=VMEM_LIMIT),
        name="sb_prompt",
    )(q16, k16, v16)


def _block_scores(qf, kmean_rows):
    pad = jnp.zeros((LANES - kmean_rows.shape[0], kmean_rows.shape[1]), jnp.float32)
    km = jnp.concatenate([kmean_rows, pad], axis=0)
    qh, ql = _split2(qf)
    kh, kl = _split2(km)
    return _dot_nt(qh, kh) + _dot_nt(qh, kl) + _dot_nt(ql, kh)


def _select_bias(s, n_past, n_blocks):
    lane = lax.broadcasted_iota(jnp.int32, s.shape, 1)
    rank = jnp.zeros(s.shape, jnp.float32)
    for m in range(n_blocks):
        sm = s[:, m:m + 1]
        beats = (sm > s) | ((sm == s) & (m < lane))
        rank = rank + jnp.where(beats, 1.0, 0.0) * jnp.where(m < n_past, 1.0, 0.0)
    sel = (lane < n_past) & (rank < float(MOBA_TOPK))
    return jnp.where(sel, 0.0, -jnp.inf)


def _select_bias_packed(s, n_past, n_blocks):
    lane = lax.broadcasted_iota(jnp.int32, s.shape, 1)
    n = lane % n_blocks
    rank = jnp.zeros(s.shape, jnp.float32)
    for d in range(1, n_blocks):
        wrap = n + d >= n_blocks
        m = jnp.where(wrap, n + d - n_blocks, n + d)
        sm = jnp.where(wrap, pltpu.roll(s, n_blocks - d, 1), pltpu.roll(s, LANES - d, 1))
        beats = (sm > s) | ((sm == s) & wrap)
        rank = rank + jnp.where(beats & (m < n_past), 1.0, 0.0)
    sel = (n < n_past) & (rank < float(MOBA_TOPK))
    return jnp.where(sel, 0.0, MASKED)


def _softmax_step(logits, pv, carry):
    m, l, acc = carry
    m_new = jnp.maximum(m, jnp.max(logits, axis=1, keepdims=True))
    alpha = jnp.exp(m - m_new)
    p = jnp.exp(logits - m_new)
    l = alpha * l + jnp.sum(p, axis=1, keepdims=True)
    acc = alpha * acc + pv(p.astype(jnp.bfloat16))
    return m_new, l, acc


def _moba_prompt_kernel(q_ref, k_ref, v_ref, kmean_ref, o_ref,
                        q2_ref, neg_ref, m_ref, l_ref, acc_ref):
    qi = pl.program_id(1)
    per_block = MOBA_BLOCK // Q_TILE
    own = qi // per_block
    rows = HEAD_PAIR * Q_TILE
    n_blocks = kmean_ref.shape[1]
    pairs = range(N_PAIRS)
    lanes = [slice(p * LANES, (p + 1) * LANES) for p in pairs]
    r = lax.broadcasted_iota(jnp.int32, (rows, MOBA_BLOCK), 0)
    c = lax.broadcasted_iota(jnp.int32, (rows, MOBA_BLOCK), 1)
    own_mask = c <= r % Q_TILE + (qi % per_block) * Q_TILE
    kmean = kmean_ref[0]

    scores = jnp.zeros((rows, LANES), jnp.float32)
    for p in pairs:
        qf = q_ref[:, lanes[p]]
        q2 = _stack_pair(qf, jnp.zeros_like(qf))
        q2_ref[p] = (q2 * Q_SCALE).astype(jnp.bfloat16)
        km = jnp.concatenate(
            [jnp.zeros((n_blocks, LANES), jnp.float32)] * p + [kmean[:, lanes[p]]]
            + [jnp.zeros((LANES - (p + 1) * n_blocks, LANES), jnp.float32)], axis=0)
        qh, ql = _split2(q2)
        kh, kl = _split2(km)
        scores = scores + (_dot_nt(qh, kh) + _dot_nt(qh, kl) + _dot_nt(ql, kh))
    neg_ref[...] = _select_bias_packed(scores, own, n_blocks).astype(jnp.bfloat16)
    ones = jnp.ones((MOBA_BLOCK, LANES), jnp.bfloat16)
    sub16 = lax.broadcasted_iota(jnp.int32, (16, LANES), 1)

    def block(n, is_own):
        start = pl.multiple_of(n * MOBA_BLOCK, MOBA_BLOCK)
        kp = [k_ref[pl.ds(start, MOBA_BLOCK), lanes[p]] for p in pairs]
        vp = [jnp.concatenate([v_ref[pl.ds(start, MOBA_BLOCK), lanes[p]], ones], axis=1)
              for p in pairs]
        if is_own:
            logits = [jnp.where(own_mask, _dot_nt(q2_ref[p], kp[p]), -jnp.inf) for p in pairs]
            m_new = [jnp.broadcast_to(jnp.max(logits[p], axis=1, keepdims=True), (rows, LANES))
                     for p in pairs]
        else:
            neg = neg_ref[...]
            pick = [jnp.concatenate(
                [jnp.where(sub16 == p * n_blocks + n, 1.0, 0.0).astype(jnp.bfloat16)]
                * (MOBA_BLOCK // 16), axis=0) for p in pairs]
            logits = [_dot_nt(jnp.concatenate([q2_ref[p], neg], axis=1),
                              jnp.concatenate([kp[p], pick[p]], axis=1)) for p in pairs]
            m_old = [m_ref[p] for p in pairs]
            m_new = [jnp.maximum(m_old[p], jnp.max(logits[p], axis=1, keepdims=True))
                     for p in pairs]
        pr = [jnp.exp(logits[p] - jnp.concatenate([m_new[p]] * (MOBA_BLOCK // LANES), axis=1))
              for p in pairs]
        pv = [_dot(pr[p].astype(jnp.bfloat16), vp[p]) for p in pairs]
        for p in pairs:
            if is_own:
                l_ref[p] = pv[p][:, LANES:]
                acc_ref[p] = pv[p][:, :LANES]
            else:
                alpha = jnp.exp(m_old[p] - m_new[p])
                l_ref[p] = alpha * l_ref[p] + pv[p][:, LANES:]
                acc_ref[p] = alpha * acc_ref[p] + pv[p][:, :LANES]
            m_ref[p] = m_new[p]

    block(own, True)

    def body(n, carry):
        block(n, False)
        return carry

    lax.fori_loop(0, own, body, 0)
    for p in range(N_PAIRS):
        o_ref[:, p * LANES:(p + 1) * LANES] = _unstack_pair(
            acc_ref[p] / l_ref[p]).astype(o_ref.dtype)


def _moba_prompt(qf, k16, v16, kmean, batch, seq):
    nq = seq // Q_TILE
    nb = seq // MOBA_BLOCK
    qspec = pl.BlockSpec((Q_TILE, WIDTH), lambda b, i: (b * nq + i, 0))
    kvspec = pl.BlockSpec((seq, WIDTH), lambda b, i: (b, 0))
    stacked = (N_PAIRS, HEAD_PAIR * Q_TILE, LANES)
    return pl.pallas_call(
        _moba_prompt_kernel,
        grid=(batch, nq),
        in_specs=[qspec, kvspec, kvspec,
                  pl.BlockSpec((1, nb, WIDTH), lambda b, i: (b, 0, 0))],
        out_specs=qspec,
        out_shape=jax.ShapeDtypeStruct((batch * seq, WIDTH), jnp.bfloat16),
        scratch_shapes=[pltpu.VMEM(stacked, jnp.bfloat16),
                        pltpu.VMEM(stacked[1:], jnp.bfloat16)]
        + [pltpu.VMEM(stacked, jnp.float32)] * 3,
        compiler_params=pltpu.CompilerParams(
            dimension_semantics=("arbitrary", "arbitrary"), vmem_limit_bytes=VMEM_LIMIT),
        name="moba_prompt",
    )(qf, k16, v16, kmean)


def _decode_kernel(pt_ref, q_ref, knew_ref, vnew_ref, *refs, moba, n_pages, n_new):
    del pt_ref
    k_pages, v_pages, o_ref = refs[:n_pages], refs[n_pages:2 * n_pages], refs[2 * n_pages]
    rows = n_new * N_HEADS
    q = q_ref[0]
    sub = lax.broadcasted_iota(jnp.int32, (N_HEADS, WIDTH), 0)
    ln = lax.broadcasted_iota(jnp.int32, (N_HEADS, WIDTH), 1)
    head_mask = (ln // HEAD_DIM) == sub
    qbd = jnp.concatenate(
        [jnp.where(head_mask, jnp.broadcast_to(q[j:j + 1, :], (N_HEADS, WIDTH)), 0.0)
         for j in range(n_new)], axis=0)
    r = lax.broadcasted_iota(jnp.int32, (rows, PAGE_SIZE), 0)
    c = lax.broadcasted_iota(jnp.int32, (rows, PAGE_SIZE), 1)
    zpad = jnp.zeros((PAGE_SIZE - n_new, WIDTH), jnp.float32)
    k_new = jnp.concatenate([knew_ref[0], zpad], axis=0).astype(jnp.bfloat16)
    v_new = jnp.concatenate([vnew_ref[0], zpad], axis=0).astype(jnp.bfloat16)
    page16 = lambda ref: ref[0].astype(jnp.bfloat16)

    if moba:
        qb = (qbd * Q_SCALE).astype(jnp.bfloat16)
        per_block = MOBA_BLOCK // PAGE_SIZE
        n_blocks = n_pages // per_block
        lane = lax.broadcasted_iota(jnp.int32, (WIDTH, LANES), 1)
        kmean_t = jnp.zeros((WIDTH, LANES), jnp.float32)
        for n in range(n_blocks):
            col = sum(jnp.sum(k_pages[n * per_block + i][0], axis=1, keepdims=True)
                      for i in range(per_block)) * (1.0 / MOBA_BLOCK)
            kmean_t = jnp.where(lane == n, col, kmean_t)
        qh, ql = _split2(qbd)
        kh, kl = _split2(kmean_t)
        scores = _dot(qh, kh) + _dot(qh, kl) + _dot(ql, kh)
        bias = _select_bias(scores, n_blocks, n_blocks)
        carry = (jnp.full((rows, 1), -jnp.inf, jnp.float32),
                 jnp.zeros((rows, 1), jnp.float32),
                 jnp.zeros((rows, WIDTH), jnp.float32))
        logits = jnp.where(c <= r // N_HEADS, _dot_nt(qb, k_new), -jnp.inf)
        carry = _softmax_step(logits, lambda p: _dot(p, v_new), carry)
        for p in range(n_pages):
            n = p // per_block
            logits = _dot(qb, page16(k_pages[p])) + bias[:, n:n + 1]
            vt = page16(v_pages[p])
            carry = _softmax_step(logits, lambda pr, vt=vt: _dot_nt(pr, vt), carry)
        m, l, acc = carry
        acc = acc / l
    else:
        qb = qbd.astype(jnp.bfloat16)
        umat = _suffix_matrix()
        new_mask = c < r // N_HEADS

        def chunk(z, av, mask, carry):
            run, acc = carry
            ls = _log_sigmoid(z)
            lk = ls - z
            if mask is not None:
                lk = jnp.where(mask, lk, 0.0)
            s = _suffix_sums(lk, umat)
            a = jnp.exp(ls + s[:, :PAGE_SIZE] + run)
            if mask is not None:
                a = jnp.where(mask, a, 0.0)
            return run + s[:, PAGE_SIZE:], acc + av(a.astype(jnp.bfloat16))

        carry = (jnp.zeros((rows, PAGE_SIZE), jnp.float32),
                 jnp.zeros((rows, WIDTH), jnp.float32))
        carry = chunk(_dot_nt(qb, k_new), lambda a: _dot(a, v_new), new_mask, carry)
        for p in reversed(range(n_pages)):
            vt = page16(v_pages[p])
            carry = chunk(_dot(qb, page16(k_pages[p])),
                          lambda a, vt=vt: _dot_nt(a, vt), None, carry)
        acc = carry[1]

    o_ref[0] = jnp.concatenate(
        [jnp.sum(jnp.where(head_mask, acc[j * N_HEADS:(j + 1) * N_HEADS, :], 0.0),
                 axis=0, keepdims=True) for j in range(n_new)], axis=0)


def _decode(page_table, q, k_new, v_new, cache_k, cache_v, moba):
    b, n_new, _ = q.shape
    n_pages = page_table.shape[1]
    tok_spec = pl.BlockSpec((1, n_new, WIDTH), lambda i, pt: (i, 0, 0))

    def page_spec(p):
        return pl.BlockSpec((1, WIDTH, PAGE_SIZE), lambda i, pt: (pt[i, p], 0, 0))

    grid_spec = pltpu.PrefetchScalarGridSpec(
        num_scalar_prefetch=1,
        grid=(b,),
        in_specs=[tok_spec, tok_spec, tok_spec]
        + [page_spec(p) for p in range(n_pages)] * 2,
        out_specs=tok_spec,
    )
    return pl.pallas_call(
        functools.partial(_decode_kernel, moba=moba, n_pages=n_pages, n_new=n_new),
        grid_spec=grid_spec,
        out_shape=jax.ShapeDtypeStruct((b, n_new, WIDTH), jnp.float32),
        compiler_params=pltpu.CompilerParams(
            dimension_semantics=("arbitrary",), vmem_limit_bytes=VMEM_LIMIT),
        name="moba_decode" if moba else "sb_decode",
    )(page_table, q, k_new, v_new, *([cache_k] * n_pages), *([cache_v] * n_pages))


def _route(lg):
    lane = lax.broadcasted_iota(jnp.int32, lg.shape, 1)
    lane_f = lane.astype(jnp.float32)
    far = float(LANES)
    is_group = lane < N_GROUPS
    gl = jnp.where(is_group, lg, -jnp.inf)
    gmax = jnp.max(gl, axis=1, keepdims=True)
    gsum = jnp.sum(jnp.where(is_group, jnp.exp(gl - gmax), 0.0), axis=1, keepdims=True)
    g_p = 1.0 / gsum
    g_i = jnp.min(jnp.where(gl == gmax, lane_f, far), axis=1, keepdims=True)
    lo = N_GROUPS + EXPERTS_PER_GROUP * g_i
    in_group = (lane_f >= lo) & (lane_f < lo + EXPERTS_PER_GROUP)
    el = jnp.where(in_group, lg, -jnp.inf)
    m1 = jnp.max(el, axis=1, keepdims=True)
    i1 = jnp.min(jnp.where(el == m1, lane_f, far), axis=1, keepdims=True)
    el2 = jnp.where(lane_f == i1, -jnp.inf, el)
    m2 = jnp.max(el2, axis=1, keepdims=True)
    i2 = jnp.min(jnp.where(el2 == m2, lane_f, far), axis=1, keepdims=True)
    t = jnp.exp(m2 - m1)
    w1 = g_p / (1.0 + t)
    w2 = w1 * t
    return jnp.where(lane_f == i1, w1, 0.0) + jnp.where(lane_f == i2, w2, 0.0)


def _merge_kernel(x_ref, osb_ref, om_ref, ga_ref, gb_ref, wps_ref, wpm_ref, wout_ref,
                  g_ref, wrh_ref, wrl_ref, br_ref, hp_ref, xn_ref, comb_ref):
    b_sb = _dot(osb_ref[...].astype(jnp.bfloat16), wps_ref[...])
    b_m = _dot(om_ref[...].astype(jnp.bfloat16), wpm_ref[...])
    m = jax.nn.sigmoid(ga_ref[...]) * b_sb + jax.nn.sigmoid(gb_ref[...]) * b_m
    hp = x_ref[...] + _dot(m.astype(jnp.bfloat16), wout_ref[...])
    hp_ref[...] = hp
    xn = _rms_norm(hp, g_ref[...])
    xn_ref[...] = xn.astype(jnp.bfloat16)
    xh, xl = _split2(xn)
    wh, wl = wrh_ref[...], wrl_ref[...]
    lg = _dot(xh, wh) + _dot(xh, wl) + _dot(xl, wh) + br_ref[...]
    comb_ref[...] = _route(lg)


def _merge(x, o_sb, o_m, ga, gb, wps, wpm, wout, g_ffn, wr_hi, wr_lo, br, tm):
    t = x.shape[0]
    row = lambda i: (i, 0)
    fixed = lambda i: (0, 0)
    return pl.pallas_call(
        _merge_kernel,
        grid=(t // tm,),
        in_specs=[pl.BlockSpec((tm, D_MODEL), row),
                  pl.BlockSpec((tm, WIDTH), row),
                  pl.BlockSpec((tm, WIDTH), row),
                  pl.BlockSpec((tm, D_MODEL), row),
                  pl.BlockSpec((tm, D_MODEL), row),
                  pl.BlockSpec((WIDTH, D_MODEL), fixed),
                  pl.BlockSpec((WIDTH, D_MODEL), fixed),
                  pl.BlockSpec((D_MODEL, D_MODEL), fixed),
                  pl.BlockSpec((1, D_MODEL), fixed),
                  pl.BlockSpec((D_MODEL, LANES), fixed),
                  pl.BlockSpec((D_MODEL, LANES), fixed),
                  pl.BlockSpec((1, LANES), fixed)],
        out_specs=(pl.BlockSpec((tm, D_MODEL), row),
                   pl.BlockSpec((tm, D_MODEL), row),
                   pl.BlockSpec((tm, LANES), row)),
        out_shape=(jax.ShapeDtypeStruct((t, D_MODEL), jnp.float32),
                   jax.ShapeDtypeStruct((t, D_MODEL), jnp.bfloat16),
                   jax.ShapeDtypeStruct((t, LANES), jnp.float32)),
        compiler_params=pltpu.CompilerParams(
            dimension_semantics=("arbitrary",), vmem_limit_bytes=VMEM_LIMIT),
        name="merge",
    )(x, o_sb, o_m, ga, gb, wps, wpm, wout, g_ffn, wr_hi, wr_lo, br)


def _moe_kernel(xn_ref, comb_ref, hp_ref, wg_ref, wu_ref, wd_ref, gf_ref, y_ref, acc_ref):
    e = pl.program_id(1)
    xn = xn_ref[...]
    a = _dot(xn, wg_ref[0])
    u = _dot(xn, wu_ref[0])
    comb = comb_ref[...]
    lane = lax.broadcasted_iota(jnp.int32, comb.shape, 1)
    w = jnp.sum(jnp.where(lane == e + N_GROUPS, comb, 0.0), axis=1, keepdims=True)
    hdn = (a * jax.nn.sigmoid(a)) * u * w
    part = _dot(hdn.astype(jnp.bfloat16), wd_ref[0])

    @pl.when(e == 0)
    def _():
        acc_ref[...] = part

    @pl.when(e > 0)
    def _():
        acc_ref[...] += part

    @pl.when(e == N_EXPERTS - 1)
    def _():
        y_ref[...] = _rms_norm(hp_ref[...] + acc_ref[...], gf_ref[...])


def _moe(xn16, comb, hp, wg, wu, wd, g_final, tm):
    t = xn16.shape[0]
    row = lambda i, e: (i, 0)
    return pl.pallas_call(
        _moe_kernel,
        grid=(t // tm, N_EXPERTS),
        in_specs=[pl.BlockSpec((tm, D_MODEL), row),
                  pl.BlockSpec((tm, LANES), row),
                  pl.BlockSpec((tm, D_MODEL), row),
                  pl.BlockSpec((1, D_MODEL, D_EXPERT), lambda i, e: (e, 0, 0)),
                  pl.BlockSpec((1, D_MODEL, D_EXPERT), lambda i, e: (e, 0, 0)),
                  pl.BlockSpec((1, D_EXPERT, D_MODEL), lambda i, e: (e, 0, 0)),
                  pl.BlockSpec((1, D_MODEL), lambda i, e: (0, 0))],
        out_specs=pl.BlockSpec((tm, D_MODEL), row),
        out_shape=jax.ShapeDtypeStruct((t, D_MODEL), jnp.float32),
        scratch_shapes=[pltpu.VMEM((tm, D_MODEL), jnp.float32)],
        compiler_params=pltpu.CompilerParams(
            dimension_semantics=("arbitrary", "arbitrary"), vmem_limit_bytes=VMEM_LIMIT),
        name="moe",
    )(xn16, comb, hp, wg, wu, wd, g_final)


def _tail(x, o_sb, o_m, ga, gb, lw, g_out, tm_merge, tm_moe):
    hp, xn16, comb = _merge(x, o_sb, o_m, ga, gb, lw["wps"], lw["wpm"], lw["wout"],
                            lw["g_ffn"], lw["wr_hi"], lw["wr_lo"], lw["br"], tm_merge)
    return _moe(xn16, comb, hp, lw["wg"], lw["wu"], lw["wd"], g_out, tm_moe)


def kernel(x_prompt, x_sample, cache_sb_k, cache_sb_v, cache_moba_k, cache_moba_v, page_table,
           g_mix, w_in, w_proj_sb, w_proj_moba, w_out, g_ffn, w_router_group, b_router_group,
           w_router_expert, b_router_expert, w_expert_gate, w_expert_up, w_expert_down, g_final):
    batch, seq, _ = x_prompt.shape
    dec_b, dec_seq, _ = x_sample.shape
    depth = w_in.shape[0]
    assert depth == 1, "the final RMSNorm is fused into the last layer's MoE kernel"
    n_phys = cache_sb_k.shape[1]
    past_len = page_table.shape[1] * PAGE_SIZE
    bf16 = jnp.bfloat16
    tm_p = 256
    tm_s = min(256, dec_b * dec_seq)

    tab_p = _rope_tables(jnp.arange(tm_p * (seq // tm_p)))
    tab_s = _rope_tables(past_len + (jnp.arange(tm_s) % dec_seq))

    hp = x_prompt.reshape(batch * seq, D_MODEL)
    hs = x_sample.reshape(dec_b * dec_seq, D_MODEL)
    outs = [[] for _ in range(8)]
    for layer in range(depth):
        w_r = jnp.concatenate([w_router_group[layer], w_router_expert[layer]], axis=1)
        w_r = jnp.pad(w_r, ((0, 0), (0, LANES - w_r.shape[1])))
        wr_hi = w_r.astype(bf16)
        b_r = jnp.concatenate([b_router_group[layer], b_router_expert[layer]])
        lw = dict(
            wps=w_proj_sb[layer].astype(bf16), wpm=w_proj_moba[layer].astype(bf16),
            wout=w_out[layer].astype(bf16), g_ffn=g_ffn[layer][None, :],
            wr_hi=wr_hi, wr_lo=(w_r - wr_hi.astype(jnp.float32)).astype(bf16),
            br=jnp.pad(b_r, (0, LANES - b_r.shape[0]))[None, :],
            wg=w_expert_gate[layer].astype(bf16), wu=w_expert_up[layer].astype(bf16),
            wd=w_expert_down[layer].astype(bf16))
        w16 = w_in[layer].astype(bf16)
        g = g_mix[layer][None, :]
        g_out = g_final[None, :]

        (q_sb, k_sb, v_sb, q_m, k_m, v_m, ga, gb,
         k_sb16, v_sb16, k_m16, v_m16, kmean) = _inproj(hp, g, w16, tab_p, tm_p)
        o_sb = _sb_prompt(q_sb, k_sb16, v_sb16, batch, seq)
        o_m = _moba_prompt(q_m, k_m16, v_m16,
                           kmean.reshape(batch, seq // MOBA_BLOCK, WIDTH), batch, seq)
        hp = _tail(hp, o_sb, o_m, ga, gb, lw, g_out, 256, 1024)
        for dst, val in zip(outs[:4], (k_sb, v_sb, k_m, v_m)):
            dst.append(val.reshape(batch, seq, N_HEADS, HEAD_DIM))

        (q_sb, k_sb, v_sb, q_m, k_m, v_m, ga, gb, *_) = _inproj(hs, g, w16, tab_s, tm_s)
        tok = lambda a: a.astype(jnp.float32).reshape(dec_b, dec_seq, WIDTH)
        pages = lambda cch: jnp.transpose(cch[layer], (0, 2, 3, 1)).reshape(
            n_phys, WIDTH, PAGE_SIZE)
        o_sb = _decode(page_table, tok(q_sb), tok(k_sb), tok(v_sb),
                       pages(cache_sb_k), pages(cache_sb_v), moba=False)
        o_m = _decode(page_table, tok(q_m), tok(k_m), tok(v_m),
                      pages(cache_moba_k), pages(cache_moba_v), moba=True)
        flat = lambda a: a.reshape(dec_b * dec_seq, WIDTH)
        hs = _tail(hs, flat(o_sb), flat(o_m), ga, gb, lw, g_out, tm_s, tm_s * 2)
        for dst, val in zip(outs[4:], (k_sb, v_sb, k_m, v_m)):
            dst.append(val.reshape(dec_b, dec_seq, N_HEADS, HEAD_DIM))

    y_prompt = hp.reshape(batch, seq, D_MODEL)
    y_sample = hs.reshape(dec_b, dec_seq, D_MODEL)
    return (y_prompt, y_sample) + tuple(jnp.stack(o) for o in outs)
```

```python
import functools

import jax
import jax.numpy as jnp
from jax import lax
from jax.experimental import pallas as pl
from jax.experimental.pallas import tpu as pltpu

D_MODEL = 1024
HEAD_DIM = 64
N_HEADS = 8
WIDTH = N_HEADS * HEAD_DIM
IN_WIDTH = 6 * WIDTH + 2 * D_MODEL
ROPE_DIM = HEAD_DIM // 4
ROPE_HALF = ROPE_DIM // 2
ROPE_THETA = 500000.0
PAGE_SIZE = 128
MOBA_BLOCK = 256
MOBA_TOPK = 3
N_GROUPS = 4
EXPERTS_PER_GROUP = 4
N_EXPERTS = N_GROUPS * EXPERTS_PER_GROUP
D_EXPERT = 256
NORM_EPS = 1e-6
Q_SCALE = HEAD_DIM ** -0.5

LANES = 128
HEAD_PAIR = LANES // HEAD_DIM
N_PAIRS = N_HEADS // HEAD_PAIR
ATT_BLOCK = 128
Q_TILE = 256
MOE_ROW_CHUNK = 256
VMEM_LIMIT = 56 * 1024 * 1024

MASKED = -1e30
EXP_UNDERFLOW = -104.0

_NT = (((1,), (1,)), ((), ()))


def _dot(a, b):
    return jnp.dot(a, b, preferred_element_type=jnp.float32)


def _dot_nt(a, b):
    return lax.dot_general(a, b, _NT, preferred_element_type=jnp.float32)


def _split2(x):
    hi = x.astype(jnp.bfloat16)
    lo = (x - hi.astype(jnp.float32)).astype(jnp.bfloat16)
    return hi, lo


def _split3(x):
    hi = x.astype(jnp.bfloat16)
    r = x - hi.astype(jnp.float32)
    mid = r.astype(jnp.bfloat16)
    lo = (r - mid.astype(jnp.float32)).astype(jnp.bfloat16)
    return hi, mid, lo


def _log_sigmoid(z):
    return jnp.minimum(z, 0.0) - jnp.log(1.0 + jnp.exp(-jnp.abs(z)))


def _suffix_matrix():
    r = lax.broadcasted_iota(jnp.int32, (3 * ATT_BLOCK, 2 * ATT_BLOCK), 0) % ATT_BLOCK
    c = lax.broadcasted_iota(jnp.int32, (3 * ATT_BLOCK, 2 * ATT_BLOCK), 1)
    return jnp.where((c >= ATT_BLOCK) | (r > c), 1.0, 0.0).astype(jnp.bfloat16)


def _suffix_sums(lk, umat):
    return _dot(jnp.concatenate(_split3(lk), axis=1), umat)


def _rms_norm(x, g):
    r = lax.rsqrt(jnp.mean(x * x, axis=-1, keepdims=True) + NORM_EPS)
    return (x * r) * g


def _rope(h, ca, cb, cc):
    outs = []
    for g in range(WIDTH // LANES):
        xg = h[:, g * LANES:(g + 1) * LANES]
        outs.append(xg * ca + pltpu.roll(xg, LANES - ROPE_HALF, 1) * cb
                    + pltpu.roll(xg, ROPE_HALF, 1) * cc)
    return jnp.concatenate(outs, axis=1)


def _inproj_kernel(x_ref, g_ref, w_ref, ca_ref, cb_ref, cc_ref,
                   qsb_ref, ksb_ref, vsb_ref, qm_ref, km_ref, vm_ref, ga_ref, gb_ref,
                   ksb16_ref, vsb16_ref, km16_ref, vm16_ref, kmean_ref, *, transposed):
    xn = _rms_norm(x_ref[...], g_ref[...]).astype(jnp.bfloat16)
    ca, cb, cc = ca_ref[...], cb_ref[...], cc_ref[...]

    def proj(c, width=WIDTH):
        return _dot(xn, w_ref[:, c:c + width])

    def emit(ref, ref16, val):
        if transposed:
            ref[0] = val.T
        else:
            ref[...] = val
        ref16[...] = val.astype(jnp.bfloat16)

    qsb_ref[...] = (proj(0) * Q_SCALE).astype(jnp.bfloat16)
    emit(ksb_ref, ksb16_ref, proj(WIDTH))
    emit(vsb_ref, vsb16_ref, proj(2 * WIDTH))
    qm_ref[...] = _rope(proj(3 * WIDTH), ca, cb, cc)
    k = _rope(proj(4 * WIDTH), ca, cb, cc)
    emit(km_ref, km16_ref, k)
    kmean_ref[0] = jnp.sum(k, axis=0, keepdims=True) * (1.0 / MOBA_BLOCK)
    emit(vm_ref, vm16_ref, proj(5 * WIDTH))
    for half in range(2):
        ga_ref[:, half * WIDTH:(half + 1) * WIDTH] = proj(6 * WIDTH + half * WIDTH)
        gb_ref[:, half * WIDTH:(half + 1) * WIDTH] = proj(6 * WIDTH + D_MODEL + half * WIDTH)


def _rope_tables(pos):
    inv = jnp.float32(ROPE_THETA) ** (-jnp.arange(ROPE_HALF, dtype=jnp.float32) / ROPE_HALF)
    ang = pos.astype(jnp.float32)[:, None] * inv[None, :]
    cos, sin = jnp.cos(ang), jnp.sin(ang)
    n = pos.shape[0]
    rest = HEAD_DIM - ROPE_DIM
    one = jnp.ones((n, rest), jnp.float32)
    zero = jnp.zeros((n, rest), jnp.float32)
    zh = jnp.zeros((n, ROPE_HALF), jnp.float32)
    a = jnp.concatenate([cos, cos, one], axis=1)
    b = jnp.concatenate([-sin, zh, zero], axis=1)
    c = jnp.concatenate([zh, sin, zero], axis=1)
    rep = LANES // HEAD_DIM
    return tuple(jnp.tile(t, (1, rep)) for t in (a, b, c))


def _inproj(x, g, w16, tables, tm, seq=None):
    t = x.shape[0]
    nt = tables[0].shape[0] // tm
    row = lambda i: (i, 0)
    fixed = lambda i: (0, 0)
    f32, bf16 = jnp.float32, jnp.bfloat16
    slab = lambda dt, w=WIDTH: jax.ShapeDtypeStruct((t, w), dt)
    if seq is None:
        kv_shape, kv_spec = slab(f32), pl.BlockSpec((tm, WIDTH), row)
    else:
        assert seq == nt * tm
        kv_shape = jax.ShapeDtypeStruct((t // seq, WIDTH, seq), f32)
        kv_spec = pl.BlockSpec((1, WIDTH, tm), lambda i: (i // nt, 0, i % nt))
    out_shape = (slab(bf16), kv_shape, kv_shape, slab(f32), kv_shape, kv_shape,
                 slab(f32, D_MODEL), slab(f32, D_MODEL),
                 slab(bf16), slab(bf16), slab(bf16), slab(bf16),
                 jax.ShapeDtypeStruct((t // tm, 1, WIDTH), f32))
    tok_spec = pl.BlockSpec((tm, WIDTH), row)
    out_specs = tuple([tok_spec, kv_spec, kv_spec, tok_spec, kv_spec, kv_spec]
                      + [pl.BlockSpec((tm, D_MODEL), row)] * 2
                      + [tok_spec] * 4
                      + [pl.BlockSpec((1, 1, WIDTH), lambda i: (i, 0, 0))])
    tab_spec = pl.BlockSpec((tm, LANES), lambda i: (i % nt, 0))
    return pl.pallas_call(
        functools.partial(_inproj_kernel, transposed=seq is not None),
        grid=(t // tm,),
        in_specs=[pl.BlockSpec((tm, D_MODEL), row),
                  pl.BlockSpec((1, D_MODEL), fixed),
                  pl.BlockSpec((D_MODEL, IN_WIDTH), fixed),
                  tab_spec, tab_spec, tab_spec],
        out_specs=out_specs,
        out_shape=out_shape,
        compiler_params=pltpu.CompilerParams(
            dimension_semantics=("arbitrary",), vmem_limit_bytes=VMEM_LIMIT),
        name="inproj",
    )(x, g, w16, *tables)


def _stack_pair(x, zero):
    low = lax.broadcasted_iota(jnp.int32, x.shape, 1) < HEAD_DIM
    return jnp.concatenate([jnp.where(low, x, zero), jnp.where(low, zero, x)], axis=0)


def _unstack_pair(y):
    half = y.shape[0] // 2
    low = lax.broadcasted_iota(jnp.int32, (half, LANES), 1) < HEAD_DIM
    return jnp.where(low, y[:half], y[half:])


def _sb_prompt_kernel(q_ref, k_ref, v_ref, o_ref, q2_ref, run_ref, acc_ref):
    qi = pl.program_id(1)
    umat = _suffix_matrix()
    rows = HEAD_PAIR * Q_TILE
    r = lax.broadcasted_iota(jnp.int32, (rows, ATT_BLOCK), 0) % Q_TILE
    c = lax.broadcasted_iota(jnp.int32, (rows, ATT_BLOCK), 1)
    pairs = range(N_PAIRS)
    lanes = [slice(p * LANES, (p + 1) * LANES) for p in pairs]
    for p in pairs:
        qp = q_ref[:, lanes[p]]
        q2_ref[p] = _stack_pair(qp, jnp.zeros_like(qp))

    def tile(kb, first, masked):
        start = pl.multiple_of(kb * ATT_BLOCK, ATT_BLOCK)
        if masked:
            causal = kb * ATT_BLOCK + c < qi * Q_TILE + r
        z = [_dot_nt(q2_ref[p], k_ref[pl.ds(start, ATT_BLOCK), lanes[p]]) for p in pairs]
        ls = [_log_sigmoid(z[p]) for p in pairs]
        lk = [ls[p] - z[p] for p in pairs]
        if masked:
            lk = [jnp.where(causal, lk[p], 0.0) for p in pairs]
        s = [_suffix_sums(lk[p], umat) for p in pairs]
        e = [ls[p] + s[p][:, :ATT_BLOCK] for p in pairs]
        if not first:
            e = [e[p] + run_ref[p] for p in pairs]
        a = [jnp.exp(e[p]) for p in pairs]
        if masked:
            a = [jnp.where(causal, a[p], 0.0) for p in pairs]
        pv = [_unstack_pair(_dot(a[p].astype(jnp.bfloat16),
                                 v_ref[pl.ds(start, ATT_BLOCK), lanes[p]])) for p in pairs]
        run = [s[p][:, ATT_BLOCK:] if first else run_ref[p] + s[p][:, ATT_BLOCK:] for p in pairs]
        for p in pairs:
            run_ref[p] = run[p]
            if first:
                acc_ref[p] = pv[p]
            else:
                acc_ref[p] += pv[p]
        return jnp.max(functools.reduce(jnp.maximum, run))

    n_diag = Q_TILE // ATT_BLOCK
    last = (qi + 1) * n_diag - 1
    reach = tile(last, True, True)
    for j in range(1, n_diag):
        reach = tile(last - j, False, True)

    def more(carry):
        j, reach = carry
        return jnp.logical_and(j <= last, reach > EXP_UNDERFLOW)

    def body(carry):
        j, _ = carry
        return j + 1, tile(last - j, False, False)

    lax.while_loop(more, body, (n_diag, reach))
    for p in pairs:
        o_ref[:, lanes[p]] = acc_ref[p].astype(o_ref.dtype)


def _sb_prompt(q16, k16, v16, batch, seq):
    nq = seq // Q_TILE
    qspec = pl.BlockSpec((Q_TILE, WIDTH), lambda b, i: (b * nq + i, 0))
    kvspec = pl.BlockSpec((seq, WIDTH), lambda b, i: (b, 0))
    stacked = (N_PAIRS, HEAD_PAIR * Q_TILE, LANES)
    return pl.pallas_call(
        _sb_prompt_kernel,
        grid=(batch, nq),
        in_specs=[qspec, kvspec, kvspec],
        out_specs=qspec,
        out_shape=jax.ShapeDtypeStruct((batch * seq, WIDTH), jnp.bfloat16),
        scratch_shapes=[pltpu.VMEM(stacked, jnp.bfloat16),
                        pltpu.VMEM(stacked, jnp.float32),
                        pltpu.VMEM((N_PAIRS, Q_TILE, LANES), jnp.float32)],
        compiler_params=pltpu.CompilerParams(
            dimension_semantics=("arbitrary", "arbitrary"), vmem_limit_bytes=VMEM_LIMIT),
        name="sb_prompt",
    )(q16, k16, v16)


def _block_scores(qf, kmean_rows):
    pad = jnp.zeros((LANES - kmean_rows.shape[0], kmean_rows.shape[1]), jnp.float32)
    km = jnp.concatenate([kmean_rows, pad], axis=0)
    qh, ql = _split2(qf)
    kh, kl = _split2(km)
    return _dot_nt(qh, kh) + _dot_nt(qh, kl) + _dot_nt(ql, kh)


def _select_bias(s, n_past, n_blocks):
    lane = lax.broadcasted_iota(jnp.int32, s.shape, 1)
    rank = jnp.zeros(s.shape, jnp.float32)
    for m in range(n_blocks):
        sm = s[:, m:m + 1]
        beats = (sm > s) | ((sm == s) & (m < lane))
        rank = rank + jnp.where(beats, 1.0, 0.0) * jnp.where(m < n_past, 1.0, 0.0)
    sel = (lane < n_past) & (rank < float(MOBA_TOPK))
    return jnp.where(sel, 0.0, -jnp.inf)


def _select_bias_packed(s, n_past, n_blocks):
    lane = lax.broadcasted_iota(jnp.int32, s.shape, 1)
    n = lane % n_blocks
    rank = jnp.zeros(s.shape, jnp.float32)
    for d in range(1, n_blocks):
        wrap = n + d >= n_blocks
        m = jnp.where(wrap, n + d - n_blocks, n + d)
        sm = jnp.where(wrap, pltpu.roll(s, n_blocks - d, 1), pltpu.roll(s, LANES - d, 1))
        beats = (sm > s) | ((sm == s) & wrap)
        rank = rank + jnp.where(beats & (m < n_past), 1.0, 0.0)
    sel = (n < n_past) & (rank < float(MOBA_TOPK))
    return jnp.where(sel, 0.0, MASKED)


def _moba_prompt_kernel(q_ref, k_ref, v_ref, kmean_ref, o_ref,
                        q2_ref, neg_ref, m_ref, l_ref, acc_ref):
    qi = pl.program_id(1)
    per_block = MOBA_BLOCK // Q_TILE
    own = qi // per_block
    rows = HEAD_PAIR * Q_TILE
    n_blocks = kmean_ref.shape[1]
    pairs = range(N_PAIRS)
    lanes = [slice(p * LANES, (p + 1) * LANES) for p in pairs]
    r = lax.broadcasted_iota(jnp.int32, (rows, MOBA_BLOCK), 0)
    c = lax.broadcasted_iota(jnp.int32, (rows, MOBA_BLOCK), 1)
    own_mask = c <= r % Q_TILE + (qi % per_block) * Q_TILE
    kmean = kmean_ref[0]

    scores = jnp.zeros((rows, LANES), jnp.float32)
    for p in pairs:
        qf = q_ref[:, lanes[p]]
        q2 = _stack_pair(qf, jnp.zeros_like(qf))
        q2_ref[p] = (q2 * Q_SCALE).astype(jnp.bfloat16)
        km = jnp.concatenate(
            [jnp.zeros((n_blocks, LANES), jnp.float32)] * p + [kmean[:, lanes[p]]]
            + [jnp.zeros((LANES - (p + 1) * n_blocks, LANES), jnp.float32)], axis=0)
        qh, ql = _split2(q2)
        kh, kl = _split2(km)
        scores = scores + (_dot_nt(qh, kh) + _dot_nt(qh, kl) + _dot_nt(ql, kh))
    neg_ref[...] = _select_bias_packed(scores, own, n_blocks).astype(jnp.bfloat16)
    ones = jnp.ones((MOBA_BLOCK, LANES), jnp.bfloat16)
    sub16 = lax.broadcasted_iota(jnp.int32, (16, LANES), 1)

    def block(n, is_own):
        start = pl.multiple_of(n * MOBA_BLOCK, MOBA_BLOCK)
        kp = [k_ref[pl.ds(start, MOBA_BLOCK), lanes[p]] for p in pairs]
        vp = [jnp.concatenate([v_ref[pl.ds(start, MOBA_BLOCK), lanes[p]], ones], axis=1)
              for p in pairs]
        if is_own:
            logits = [jnp.where(own_mask, _dot_nt(q2_ref[p], kp[p]), -jnp.inf) for p in pairs]
            m_new = [jnp.broadcast_to(jnp.max(logits[p], axis=1, keepdims=True), (rows, LANES))
                     for p in pairs]
        else:
            neg = neg_ref[...]
            pick = [jnp.concatenate(
                [jnp.where(sub16 == p * n_blocks + n, 1.0, 0.0).astype(jnp.bfloat16)]
                * (MOBA_BLOCK // 16), axis=0) for p in pairs]
            logits = [_dot_nt(jnp.concatenate([q2_ref[p], neg], axis=1),
                              jnp.concatenate([kp[p], pick[p]], axis=1)) for p in pairs]
            m_old = [m_ref[p] for p in pairs]
            m_new = [jnp.maximum(m_old[p], jnp.max(logits[p], axis=1, keepdims=True))
                     for p in pairs]
        pr = [jnp.exp(logits[p] - jnp.concatenate([m_new[p]] * (MOBA_BLOCK // LANES), axis=1))
              for p in pairs]
        pv = [_dot(pr[p].astype(jnp.bfloat16), vp[p]) for p in pairs]
        for p in pairs:
            if is_own:
                l_ref[p] = pv[p][:, LANES:]
                acc_ref[p] = pv[p][:, :LANES]
            else:
                alpha = jnp.exp(m_old[p] - m_new[p])
                l_ref[p] = alpha * l_ref[p] + pv[p][:, LANES:]
                acc_ref[p] = alpha * acc_ref[p] + pv[p][:, :LANES]
            m_ref[p] = m_new[p]

    block(own, True)

    def body(n, carry):
        block(n, False)
        return carry

    lax.fori_loop(0, own, body, 0)
    for p in range(N_PAIRS):
        o_ref[:, p * LANES:(p + 1) * LANES] = _unstack_pair(
            acc_ref[p] / l_ref[p]).astype(o_ref.dtype)


def _moba_prompt(qf, k16, v16, kmean, batch, seq):
    nq = seq // Q_TILE
    nb = seq // MOBA_BLOCK
    qspec = pl.BlockSpec((Q_TILE, WIDTH), lambda b, i: (b * nq + i, 0))
    kvspec = pl.BlockSpec((seq, WIDTH), lambda b, i: (b, 0))
    stacked = (N_PAIRS, HEAD_PAIR * Q_TILE, LANES)
    return pl.pallas_call(
        _moba_prompt_kernel,
        grid=(batch, nq),
        in_specs=[qspec, kvspec, kvspec,
                  pl.BlockSpec((1, nb, WIDTH), lambda b, i: (b, 0, 0))],
        out_specs=qspec,
        out_shape=jax.ShapeDtypeStruct((batch * seq, WIDTH), jnp.bfloat16),
        scratch_shapes=[pltpu.VMEM(stacked, jnp.bfloat16),
                        pltpu.VMEM(stacked[1:], jnp.bfloat16)]
        + [pltpu.VMEM(stacked, jnp.float32)] * 3,
        compiler_params=pltpu.CompilerParams(
            dimension_semantics=("arbitrary", "arbitrary"), vmem_limit_bytes=VMEM_LIMIT),
        name="moba_prompt",
    )(qf, k16, v16, kmean)


def _decode_kernel(pt_ref, q_ref, knew_ref, vnew_ref, *refs, moba, n_pages, n_new):
    del pt_ref
    k_pages, v_pages, o_ref = refs[:n_pages], refs[n_pages:2 * n_pages], refs[2 * n_pages]
    rows = n_new * N_HEADS
    q = q_ref[0]
    sub = lax.broadcasted_iota(jnp.int32, (N_HEADS, WIDTH), 0)
    ln = lax.broadcasted_iota(jnp.int32, (N_HEADS, WIDTH), 1)
    head_mask = (ln // HEAD_DIM) == sub
    qbd = jnp.concatenate(
        [jnp.where(head_mask, jnp.broadcast_to(q[j:j + 1, :], (N_HEADS, WIDTH)), 0.0)
         for j in range(n_new)], axis=0)
    r = lax.broadcasted_iota(jnp.int32, (rows, PAGE_SIZE), 0)
    c = lax.broadcasted_iota(jnp.int32, (rows, PAGE_SIZE), 1)
    zpad = jnp.zeros((PAGE_SIZE - n_new, WIDTH), jnp.float32)
    k_new = jnp.concatenate([knew_ref[0], zpad], axis=0).astype(jnp.bfloat16)
    v_new = jnp.concatenate([vnew_ref[0], zpad], axis=0).astype(jnp.bfloat16)
    page16 = lambda ref: ref[0].astype(jnp.bfloat16)
    order = list(reversed(range(n_pages)))

    if moba:
        qb = (qbd * Q_SCALE).astype(jnp.bfloat16)
        per_block = MOBA_BLOCK // PAGE_SIZE
        n_blocks = n_pages // per_block
        lane = lax.broadcasted_iota(jnp.int32, (WIDTH, LANES), 1)
        kmean_t = jnp.zeros((WIDTH, LANES), jnp.float32)
        for n in range(n_blocks):
            block_sum = sum(k_pages[n * per_block + i][0] for i in range(per_block))
            col = jnp.sum(block_sum, axis=1, keepdims=True) * (1.0 / MOBA_BLOCK)
            kmean_t = jnp.where(lane == n, col, kmean_t)
        qh, ql = _split2(qbd)
        kh, kl = _split2(kmean_t)
        scores = _dot(qh, kh) + _dot(qh, kl) + _dot(ql, kh)
        bias = _select_bias(scores, n_blocks, n_blocks)
        logits = [jnp.where(c <= r // N_HEADS, _dot_nt(qb, k_new), -jnp.inf)]
        logits += [_dot(qb, page16(k_pages[p])) + bias[:, p // per_block:p // per_block + 1]
                   for p in order]
        top = functools.reduce(jnp.maximum, logits)
        m = jnp.max(top, axis=1, keepdims=True)
        pr = [jnp.exp(lg - m) for lg in logits]
        l = jnp.sum(functools.reduce(jnp.add, pr), axis=1, keepdims=True)
        acc = _dot(pr[0].astype(jnp.bfloat16), v_new)
        for i, p in enumerate(order):
            acc = acc + _dot_nt(pr[i + 1].astype(jnp.bfloat16), page16(v_pages[p]))
        acc = acc / l
    else:
        qb = qbd.astype(jnp.bfloat16)
        umat = _suffix_matrix()
        new_mask = c < r // N_HEADS
        z = [_dot_nt(qb, k_new)] + [_dot(qb, page16(k_pages[p])) for p in order]
        ls = [_log_sigmoid(zi) for zi in z]
        lk = [lsi - zi for lsi, zi in zip(ls, z)]
        lk[0] = jnp.where(new_mask, lk[0], 0.0)
        s = _suffix_sums(jnp.concatenate(lk, axis=0), umat)
        a, run = [], None
        for i in range(len(z)):
            si = s[i * rows:(i + 1) * rows]
            e = ls[i] + si[:, :PAGE_SIZE]
            a.append(jnp.exp(e if run is None else e + run))
            run = si[:, PAGE_SIZE:] if run is None else run + si[:, PAGE_SIZE:]
        a[0] = jnp.where(new_mask, a[0], 0.0)
        acc = _dot(a[0].astype(jnp.bfloat16), v_new)
        for i, p in enumerate(order):
            acc = acc + _dot_nt(a[i + 1].astype(jnp.bfloat16), page16(v_pages[p]))

    o_ref[0] = jnp.concatenate(
        [jnp.sum(jnp.where(head_mask, acc[j * N_HEADS:(j + 1) * N_HEADS, :], 0.0),
                 axis=0, keepdims=True) for j in range(n_new)], axis=0)


def _decode(page_table, q, k_new, v_new, cache_k, cache_v, moba):
    b, n_new, _ = q.shape
    n_pages = page_table.shape[1]
    tok_spec = pl.BlockSpec((1, n_new, WIDTH), lambda i, pt: (i, 0, 0))

    def page_spec(p):
        return pl.BlockSpec((1, WIDTH, PAGE_SIZE), lambda i, pt: (pt[i, p], 0, 0))

    grid_spec = pltpu.PrefetchScalarGridSpec(
        num_scalar_prefetch=1,
        grid=(b,),
        in_specs=[tok_spec, tok_spec, tok_spec]
        + [page_spec(p) for p in range(n_pages)] * 2,
        out_specs=tok_spec,
    )
    return pl.pallas_call(
        functools.partial(_decode_kernel, moba=moba, n_pages=n_pages, n_new=n_new),
        grid_spec=grid_spec,
        out_shape=jax.ShapeDtypeStruct((b, n_new, WIDTH), jnp.float32),
        compiler_params=pltpu.CompilerParams(
            dimension_semantics=("arbitrary",), vmem_limit_bytes=VMEM_LIMIT),
        name="moba_decode" if moba else "sb_decode",
    )(page_table, q, k_new, v_new, *([cache_k] * n_pages), *([cache_v] * n_pages))


def _route(lg):
    lane = lax.broadcasted_iota(jnp.int32, lg.shape, 1)
    lane_f = lane.astype(jnp.float32)
    far = float(LANES)
    is_group = lane < N_GROUPS
    gl = jnp.where(is_group, lg, -jnp.inf)
    gmax = jnp.max(gl, axis=1, keepdims=True)
    gsum = jnp.sum(jnp.where(is_group, jnp.exp(gl - gmax), 0.0), axis=1, keepdims=True)
    g_p = 1.0 / gsum
    g_i = jnp.min(jnp.where(gl == gmax, lane_f, far), axis=1, keepdims=True)
    lo = N_GROUPS + EXPERTS_PER_GROUP * g_i
    in_group = (lane_f >= lo) & (lane_f < lo + EXPERTS_PER_GROUP)
    el = jnp.where(in_group, lg, -jnp.inf)
    m1 = jnp.max(el, axis=1, keepdims=True)
    i1 = jnp.min(jnp.where(el == m1, lane_f, far), axis=1, keepdims=True)
    el2 = jnp.where(lane_f == i1, -jnp.inf, el)
    m2 = jnp.max(el2, axis=1, keepdims=True)
    i2 = jnp.min(jnp.where(el2 == m2, lane_f, far), axis=1, keepdims=True)
    t = jnp.exp(m2 - m1)
    w1 = g_p / (1.0 + t)
    w2 = w1 * t
    return jnp.where(lane_f == i1, w1, 0.0) + jnp.where(lane_f == i2, w2, 0.0)


def _merge_kernel(x_ref, osb_ref, om_ref, ga_ref, gb_ref, wps_ref, wpm_ref, wout_ref,
                  g_ref, wrh_ref, wrl_ref, br_ref, hp_ref, xn_ref, comb_ref):
    b_sb = _dot(osb_ref[...].astype(jnp.bfloat16), wps_ref[...])
    b_m = _dot(om_ref[...].astype(jnp.bfloat16), wpm_ref[...])
    m = jax.nn.sigmoid(ga_ref[...]) * b_sb + jax.nn.sigmoid(gb_ref[...]) * b_m
    hp = x_ref[...] + _dot(m.astype(jnp.bfloat16), wout_ref[...])
    hp_ref[...] = hp
    xn = _rms_norm(hp, g_ref[...])
    xn_ref[...] = xn.astype(jnp.bfloat16)
    xh, xl = _split2(xn)
    wh, wl = wrh_ref[...], wrl_ref[...]
    lg = _dot(xh, wh) + _dot(xh, wl) + _dot(xl, wh) + br_ref[...]
    comb_ref[...] = _route(lg)


def _merge(x, o_sb, o_m, ga, gb, wps, wpm, wout, g_ffn, wr_hi, wr_lo, br, tm):
    t = x.shape[0]
    row = lambda i: (i, 0)
    fixed = lambda i: (0, 0)
    return pl.pallas_call(
        _merge_kernel,
        grid=(t // tm,),
        in_specs=[pl.BlockSpec((tm, D_MODEL), row),
                  pl.BlockSpec((tm, WIDTH), row),
                  pl.BlockSpec((tm, WIDTH), row),
                  pl.BlockSpec((tm, D_MODEL), row),
                  pl.BlockSpec((tm, D_MODEL), row),
                  pl.BlockSpec((WIDTH, D_MODEL), fixed),
                  pl.BlockSpec((WIDTH, D_MODEL), fixed),
                  pl.BlockSpec((D_MODEL, D_MODEL), fixed),
                  pl.BlockSpec((1, D_MODEL), fixed),
                  pl.BlockSpec((D_MODEL, LANES), fixed),
                  pl.BlockSpec((D_MODEL, LANES), fixed),
                  pl.BlockSpec((1, LANES), fixed)],
        out_specs=(pl.BlockSpec((tm, D_MODEL), row),
                   pl.BlockSpec((tm, D_MODEL), row),
                   pl.BlockSpec((tm, LANES), row)),
        out_shape=(jax.ShapeDtypeStruct((t, D_MODEL), jnp.float32),
                   jax.ShapeDtypeStruct((t, D_MODEL), jnp.bfloat16),
                   jax.ShapeDtypeStruct((t, LANES), jnp.float32)),
        compiler_params=pltpu.CompilerParams(
            dimension_semantics=("arbitrary",), vmem_limit_bytes=VMEM_LIMIT),
        name="merge",
    )(x, o_sb, o_m, ga, gb, wps, wpm, wout, g_ffn, wr_hi, wr_lo, br)


def _moe_kernel(xn_ref, comb_ref, hp_ref, wg_ref, wu_ref, wd_ref, gf_ref, y_ref, acc_ref):
    e = pl.program_id(1)
    tm = xn_ref.shape[0]
    rc = min(tm, MOE_ROW_CHUNK)

    @pl.when(e == 0)
    def _():
        acc_ref[...] = hp_ref[...]

    def gate_up(c):
        xc = xn_ref[c * rc:(c + 1) * rc, :]
        return _dot(xc, wg_ref[0]), _dot(xc, wu_ref[0])

    lane = lax.broadcasted_iota(jnp.int32, (rc, LANES), 1)
    nxt = gate_up(0)
    for c in range(tm // rc):
        a, u = nxt
        if c + 1 < tm // rc:
            nxt = gate_up(c + 1)
        rows = slice(c * rc, (c + 1) * rc)
        w = jnp.sum(jnp.where(lane == e + N_GROUPS, comb_ref[rows, :], 0.0),
                    axis=1, keepdims=True)
        hdn = (a * jax.nn.sigmoid(a)) * u * w
        acc_ref[rows, :] += _dot(hdn.astype(jnp.bfloat16), wd_ref[0])

    @pl.when(e == N_EXPERTS - 1)
    def _():
        y_ref[...] = _rms_norm(acc_ref[...], gf_ref[...])


def _moe(xn16, comb, hp, wg, wu, wd, g_final, tm):
    t = xn16.shape[0]
    row = lambda i, e: (i, 0)
    return pl.pallas_call(
        _moe_kernel,
        grid=(t // tm, N_EXPERTS),
        in_specs=[pl.BlockSpec((tm, D_MODEL), row),
                  pl.BlockSpec((tm, LANES), row),
                  pl.BlockSpec((tm, D_MODEL), row),
                  pl.BlockSpec((1, D_MODEL, D_EXPERT), lambda i, e: (e, 0, 0)),
                  pl.BlockSpec((1, D_MODEL, D_EXPERT), lambda i, e: (e, 0, 0)),
                  pl.BlockSpec((1, D_EXPERT, D_MODEL), lambda i, e: (e, 0, 0)),
                  pl.BlockSpec((1, D_MODEL), lambda i, e: (0, 0))],
        out_specs=pl.BlockSpec((tm, D_MODEL), row),
        out_shape=jax.ShapeDtypeStruct((t, D_MODEL), jnp.float32),
        scratch_shapes=[pltpu.VMEM((tm, D_MODEL), jnp.float32)],
        compiler_params=pltpu.CompilerParams(
            dimension_semantics=("arbitrary", "arbitrary"), vmem_limit_bytes=VMEM_LIMIT),
        name="moe",
    )(xn16, comb, hp, wg, wu, wd, g_final)


def _tail(x, o_sb, o_m, ga, gb, lw, g_out, tm_merge, tm_moe):
    hp, xn16, comb = _merge(x, o_sb, o_m, ga, gb, lw["wps"], lw["wpm"], lw["wout"],
                            lw["g_ffn"], lw["wr_hi"], lw["wr_lo"], lw["br"], tm_merge)
    return _moe(xn16, comb, hp, lw["wg"], lw["wu"], lw["wd"], g_out, tm_moe)


def kernel(x_prompt, x_sample, cache_sb_k, cache_sb_v, cache_moba_k, cache_moba_v, page_table,
           g_mix, w_in, w_proj_sb, w_proj_moba, w_out, g_ffn, w_router_group, b_router_group,
           w_router_expert, b_router_expert, w_expert_gate, w_expert_up, w_expert_down, g_final):
    batch, seq, _ = x_prompt.shape
    dec_b, dec_seq, _ = x_sample.shape
    depth = w_in.shape[0]
    assert depth == 1, "the final RMSNorm is fused into the last layer's MoE kernel"
    n_phys = cache_sb_k.shape[1]
    past_len = page_table.shape[1] * PAGE_SIZE
    bf16 = jnp.bfloat16
    tm_p = 256
    tm_s = min(256, dec_b * dec_seq)

    tab_p = _rope_tables(jnp.arange(tm_p * (seq // tm_p)))
    tab_s = _rope_tables(past_len + (jnp.arange(tm_s) % dec_seq))

    hp = x_prompt.reshape(batch * seq, D_MODEL)
    hs = x_sample.reshape(dec_b * dec_seq, D_MODEL)
    outs = [[] for _ in range(8)]
    for layer in range(depth):
        w_r = jnp.concatenate([w_router_group[layer], w_router_expert[layer]], axis=1)
        w_r = jnp.pad(w_r, ((0, 0), (0, LANES - w_r.shape[1])))
        wr_hi = w_r.astype(bf16)
        b_r = jnp.concatenate([b_router_group[layer], b_router_expert[layer]])
        lw = dict(
            wps=w_proj_sb[layer].astype(bf16), wpm=w_proj_moba[layer].astype(bf16),
            wout=w_out[layer].astype(bf16), g_ffn=g_ffn[layer][None, :],
            wr_hi=wr_hi, wr_lo=(w_r - wr_hi.astype(jnp.float32)).astype(bf16),
            br=jnp.pad(b_r, (0, LANES - b_r.shape[0]))[None, :],
            wg=w_expert_gate[layer].astype(bf16), wu=w_expert_up[layer].astype(bf16),
            wd=w_expert_down[layer].astype(bf16))
        w16 = w_in[layer].astype(bf16)
        g = g_mix[layer][None, :]
        g_out = g_final[None, :]

        (q_sb, k_sb, v_sb, q_m, k_m, v_m, ga, gb,
         k_sb16, v_sb16, k_m16, v_m16, kmean) = _inproj(hp, g, w16, tab_p, tm_p, seq)
        o_sb = _sb_prompt(q_sb, k_sb16, v_sb16, batch, seq)
        o_m = _moba_prompt(q_m, k_m16, v_m16,
                           kmean.reshape(batch, seq // MOBA_BLOCK, WIDTH), batch, seq)
        hp = _tail(hp, o_sb, o_m, ga, gb, lw, g_out, 256, 1024)
        for dst, val in zip(outs[:4], (k_sb, v_sb, k_m, v_m)):
            dst.append(jnp.transpose(val.reshape(batch, N_HEADS, HEAD_DIM, seq), (0, 3, 1, 2)))

        (q_sb, k_sb, v_sb, q_m, k_m, v_m, ga, gb, *_) = _inproj(hs, g, w16, tab_s, tm_s)
        tok = lambda a: a.astype(jnp.float32).reshape(dec_b, dec_seq, WIDTH)
        pages = lambda cch: jnp.transpose(cch[layer], (0, 2, 3, 1)).reshape(
            n_phys, WIDTH, PAGE_SIZE)
        o_sb = _decode(page_table, tok(q_sb), tok(k_sb), tok(v_sb),
                       pages(cache_sb_k), pages(cache_sb_v), moba=False)
        o_m = _decode(page_table, tok(q_m), tok(k_m), tok(v_m),
                      pages(cache_moba_k), pages(cache_moba_v), moba=True)
        flat = lambda a: a.reshape(dec_b * dec_seq, WIDTH)
        hs = _tail(hs, flat(o_sb), flat(o_m), ga, gb, lw, g_out, tm_s, tm_s * 2)
        for dst, val in zip(outs[4:], (k_sb, v_sb, k_m, v_m)):
            dst.append(val.reshape(dec_b, dec_seq, N_HEADS, HEAD_DIM))

    y_prompt = hp.reshape(batch, seq, D_MODEL)
    y_sample = hs.reshape(dec_b, dec_seq, D_MODEL)
    return (y_prompt, y_sample) + tuple(jnp.stack(o) for o in outs)
```

```python
import functools

import jax
import jax.numpy as jnp
from jax import lax
from jax.experimental import pallas as pl
from jax.experimental.pallas import tpu as pltpu

D_MODEL = 1024
HEAD_DIM = 64
N_HEADS = 8
WIDTH = N_HEADS * HEAD_DIM
IN_WIDTH = 6 * WIDTH + 2 * D_MODEL
ROPE_DIM = HEAD_DIM // 4
ROPE_HALF = ROPE_DIM // 2
ROPE_THETA = 500000.0
PAGE_SIZE = 128
MOBA_BLOCK = 256
MOBA_TOPK = 3
N_GROUPS = 4
EXPERTS_PER_GROUP = 4
N_EXPERTS = N_GROUPS * EXPERTS_PER_GROUP
D_EXPERT = 256
NORM_EPS = 1e-6
Q_SCALE = HEAD_DIM ** -0.5

LANES = 128
HEAD_PAIR = LANES // HEAD_DIM
N_PAIRS = N_HEADS // HEAD_PAIR
ATT_BLOCK = 128
Q_TILE = 256
MOE_ROW_CHUNK = 256
MOBA_PAIR_GROUP = 2
MERGE_ROW_CHUNK = 256
VMEM_LIMIT = 56 * 1024 * 1024

MASKED = -1e30
EXP_UNDERFLOW = -104.0

_NT = (((1,), (1,)), ((), ()))


def _dot(a, b):
    return jnp.dot(a, b, preferred_element_type=jnp.float32)


def _dot_nt(a, b):
    return lax.dot_general(a, b, _NT, preferred_element_type=jnp.float32)


def _split2(x):
    hi = x.astype(jnp.bfloat16)
    lo = (x - hi.astype(jnp.float32)).astype(jnp.bfloat16)
    return hi, lo


def _split3(x):
    hi = x.astype(jnp.bfloat16)
    r = x - hi.astype(jnp.float32)
    mid = r.astype(jnp.bfloat16)
    lo = (r - mid.astype(jnp.float32)).astype(jnp.bfloat16)
    return hi, mid, lo


def _log_sigmoid(z):
    return jnp.minimum(z, 0.0) - jnp.log(1.0 + jnp.exp(-jnp.abs(z)))


def _suffix_matrix():
    r = lax.broadcasted_iota(jnp.int32, (3 * ATT_BLOCK, 2 * ATT_BLOCK), 0) % ATT_BLOCK
    c = lax.broadcasted_iota(jnp.int32, (3 * ATT_BLOCK, 2 * ATT_BLOCK), 1)
    return jnp.where((c >= ATT_BLOCK) | (r > c), 1.0, 0.0).astype(jnp.bfloat16)


def _suffix_sums(lk, umat):
    return _dot(jnp.concatenate(_split3(lk), axis=1), umat)


def _rms_norm(x, g):
    r = lax.rsqrt(jnp.mean(x * x, axis=-1, keepdims=True) + NORM_EPS)
    return (x * r) * g


def _rope(h, ca, cb, cc):
    outs = []
    for g in range(WIDTH // LANES):
        xg = h[:, g * LANES:(g + 1) * LANES]
        outs.append(xg * ca + pltpu.roll(xg, LANES - ROPE_HALF, 1) * cb
                    + pltpu.roll(xg, ROPE_HALF, 1) * cc)
    return jnp.concatenate(outs, axis=1)


def _inproj_kernel(x_ref, g_ref, w_ref, ca_ref, cb_ref, cc_ref,
                   qsb_ref, ksb_ref, vsb_ref, qm_ref, km_ref, vm_ref, ga_ref, gb_ref,
                   ksb16_ref, vsb16_ref, km16_ref, vm16_ref, kmean_ref, *, transposed):
    xn = _rms_norm(x_ref[...], g_ref[...]).astype(jnp.bfloat16)
    ca, cb, cc = ca_ref[...], cb_ref[...], cc_ref[...]

    def proj(c, width=WIDTH):
        return _dot(xn, w_ref[:, c:c + width])

    def emit(ref, ref16, val):
        if transposed:
            ref[0] = val.T
        else:
            ref[...] = val
        ref16[...] = val.astype(jnp.bfloat16)

    qsb_ref[...] = (proj(0) * Q_SCALE).astype(jnp.bfloat16)
    emit(ksb_ref, ksb16_ref, proj(WIDTH))
    emit(vsb_ref, vsb16_ref, proj(2 * WIDTH))
    qm_ref[...] = _rope(proj(3 * WIDTH), ca, cb, cc)
    k = _rope(proj(4 * WIDTH), ca, cb, cc)
    emit(km_ref, km16_ref, k)
    kmean_ref[0] = jnp.sum(k, axis=0, keepdims=True) * (1.0 / MOBA_BLOCK)
    emit(vm_ref, vm16_ref, proj(5 * WIDTH))
    for half in range(2):
        ga_ref[:, half * WIDTH:(half + 1) * WIDTH] = proj(6 * WIDTH + half * WIDTH)
        gb_ref[:, half * WIDTH:(half + 1) * WIDTH] = proj(6 * WIDTH + D_MODEL + half * WIDTH)


def _rope_tables(pos):
    inv = jnp.float32(ROPE_THETA) ** (-jnp.arange(ROPE_HALF, dtype=jnp.float32) / ROPE_HALF)
    ang = pos.astype(jnp.float32)[:, None] * inv[None, :]
    cos, sin = jnp.cos(ang), jnp.sin(ang)
    n = pos.shape[0]
    rest = HEAD_DIM - ROPE_DIM
    one = jnp.ones((n, rest), jnp.float32)
    zero = jnp.zeros((n, rest), jnp.float32)
    zh = jnp.zeros((n, ROPE_HALF), jnp.float32)
    a = jnp.concatenate([cos, cos, one], axis=1)
    b = jnp.concatenate([-sin, zh, zero], axis=1)
    c = jnp.concatenate([zh, sin, zero], axis=1)
    rep = LANES // HEAD_DIM
    return tuple(jnp.tile(t, (1, rep)) for t in (a, b, c))


def _inproj(x, g, w16, tables, tm, seq=None):
    t = x.shape[0]
    nt = tables[0].shape[0] // tm
    row = lambda i: (i, 0)
    fixed = lambda i: (0, 0)
    f32, bf16 = jnp.float32, jnp.bfloat16
    slab = lambda dt, w=WIDTH: jax.ShapeDtypeStruct((t, w), dt)
    if seq is None:
        kv_shape, kv_spec = slab(f32), pl.BlockSpec((tm, WIDTH), row)
    else:
        assert seq == nt * tm
        kv_shape = jax.ShapeDtypeStruct((t // seq, WIDTH, seq), f32)
        kv_spec = pl.BlockSpec((1, WIDTH, tm), lambda i: (i // nt, 0, i % nt))
    out_shape = (slab(bf16), kv_shape, kv_shape, slab(f32), kv_shape, kv_shape,
                 slab(f32, D_MODEL), slab(f32, D_MODEL),
                 slab(bf16), slab(bf16), slab(bf16), slab(bf16),
                 jax.ShapeDtypeStruct((t // tm, 1, WIDTH), f32))
    tok_spec = pl.BlockSpec((tm, WIDTH), row)
    out_specs = tuple([tok_spec, kv_spec, kv_spec, tok_spec, kv_spec, kv_spec]
                      + [pl.BlockSpec((tm, D_MODEL), row)] * 2
                      + [tok_spec] * 4
                      + [pl.BlockSpec((1, 1, WIDTH), lambda i: (i, 0, 0))])
    tab_spec = pl.BlockSpec((tm, LANES), lambda i: (i % nt, 0))
    return pl.pallas_call(
        functools.partial(_inproj_kernel, transposed=seq is not None),
        grid=(t // tm,),
        in_specs=[pl.BlockSpec((tm, D_MODEL), row),
                  pl.BlockSpec((1, D_MODEL), fixed),
                  pl.BlockSpec((D_MODEL, IN_WIDTH), fixed),
                  tab_spec, tab_spec, tab_spec],
        out_specs=out_specs,
        out_shape=out_shape,
        compiler_params=pltpu.CompilerParams(
            dimension_semantics=("arbitrary",), vmem_limit_bytes=VMEM_LIMIT),
        name="inproj",
    )(x, g, w16, *tables)


def _stack_pair(x, zero):
    low = lax.broadcasted_iota(jnp.int32, x.shape, 1) < HEAD_DIM
    return jnp.concatenate([jnp.where(low, x, zero), jnp.where(low, zero, x)], axis=0)


def _unstack_pair(y):
    half = y.shape[0] // 2
    low = lax.broadcasted_iota(jnp.int32, (half, LANES), 1) < HEAD_DIM
    return jnp.where(low, y[:half], y[half:])


def _sb_prompt_kernel(q_ref, k_ref, v_ref, o_ref, q2_ref, run_ref, acc_ref):
    qi = pl.program_id(1)
    umat = _suffix_matrix()
    rows = HEAD_PAIR * Q_TILE
    r = lax.broadcasted_iota(jnp.int32, (rows, ATT_BLOCK), 0) % Q_TILE
    c = lax.broadcasted_iota(jnp.int32, (rows, ATT_BLOCK), 1)
    pairs = range(N_PAIRS)
    lanes = [slice(p * LANES, (p + 1) * LANES) for p in pairs]
    for p in pairs:
        qp = q_ref[:, lanes[p]]
        q2_ref[p] = _stack_pair(qp, jnp.zeros_like(qp))

    def tile(kb, first, masked):
        start = pl.multiple_of(kb * ATT_BLOCK, ATT_BLOCK)
        if masked:
            causal = kb * ATT_BLOCK + c < qi * Q_TILE + r
        z = [_dot_nt(q2_ref[p], k_ref[pl.ds(start, ATT_BLOCK), lanes[p]]) for p in pairs]
        ls = [_log_sigmoid(z[p]) for p in pairs]
        lk = [ls[p] - z[p] for p in pairs]
        if masked:
            lk = [jnp.where(causal, lk[p], 0.0) for p in pairs]
        s = [_suffix_sums(lk[p], umat) for p in pairs]
        e = [ls[p] + s[p][:, :ATT_BLOCK] for p in pairs]
        if not first:
            e = [e[p] + run_ref[p] for p in pairs]
        a = [jnp.exp(e[p]) for p in pairs]
        if masked:
            a = [jnp.where(causal, a[p], 0.0) for p in pairs]
        pv = [_unstack_pair(_dot(a[p].astype(jnp.bfloat16),
                                 v_ref[pl.ds(start, ATT_BLOCK), lanes[p]])) for p in pairs]
        run = [s[p][:, ATT_BLOCK:] if first else run_ref[p] + s[p][:, ATT_BLOCK:] for p in pairs]
        for p in pairs:
            run_ref[p] = run[p]
            if first:
                acc_ref[p] = pv[p]
            else:
                acc_ref[p] += pv[p]
        return jnp.max(functools.reduce(jnp.maximum, run))

    n_diag = Q_TILE // ATT_BLOCK
    last = (qi + 1) * n_diag - 1
    reach = tile(last, True, True)
    for j in range(1, n_diag):
        reach = tile(last - j, False, True)

    def more(carry):
        j, reach = carry
        return jnp.logical_and(j <= last, reach > EXP_UNDERFLOW)

    def body(carry):
        j, _ = carry
        return j + 1, tile(last - j, False, False)

    lax.while_loop(more, body, (n_diag, reach))
    for p in pairs:
        o_ref[:, lanes[p]] = acc_ref[p].astype(o_ref.dtype)


def _sb_prompt(q16, k16, v16, batch, seq):
    nq = seq // Q_TILE
    qspec = pl.BlockSpec((Q_TILE, WIDTH), lambda b, i: (b * nq + i, 0))
    kvspec = pl.BlockSpec((seq, WIDTH), lambda b, i: (b, 0))
    stacked = (N_PAIRS, HEAD_PAIR * Q_TILE, LANES)
    return pl.pallas_call(
        _sb_prompt_kernel,
        grid=(batch, nq),
        in_specs=[qspec, kvspec, kvspec],
        out_specs=qspec,
        out_shape=jax.ShapeDtypeStruct((batch * seq, WIDTH), jnp.bfloat16),
        scratch_shapes=[pltpu.VMEM(stacked, jnp.bfloat16),
                        pltpu.VMEM(stacked, jnp.float32),
                        pltpu.VMEM((N_PAIRS, Q_TILE, LANES), jnp.float32)],
        compiler_params=pltpu.CompilerParams(
            dimension_semantics=("arbitrary", "arbitrary"), vmem_limit_bytes=VMEM_LIMIT),
        name="sb_prompt",
    )(q16, k16, v16)


def _block_scores(qf, kmean_rows):
    pad = jnp.zeros((LANES - kmean_rows.shape[0], kmean_rows.shape[1]), jnp.float32)
    km = jnp.concatenate([kmean_rows, pad], axis=0)
    qh, ql = _split2(qf)
    kh, kl = _split2(km)
    return _dot_nt(qh, kh) + _dot_nt(qh, kl) + _dot_nt(ql, kh)


def _select_bias(s, n_past, n_blocks):
    lane = lax.broadcasted_iota(jnp.int32, s.shape, 1)
    rank = jnp.zeros(s.shape, jnp.float32)
    for m in range(n_blocks):
        sm = s[:, m:m + 1]
        beats = (sm > s) | ((sm == s) & (m < lane))
        rank = rank + jnp.where(beats, 1.0, 0.0) * jnp.where(m < n_past, 1.0, 0.0)
    sel = (lane < n_past) & (rank < float(MOBA_TOPK))
    return jnp.where(sel, 0.0, -jnp.inf)


def _select_bias_rows(s, n_past):
    n_blocks = s.shape[0]
    n = lax.broadcasted_iota(jnp.int32, s.shape, 0)
    rank = jnp.zeros(s.shape, jnp.float32)
    for d in range(1, n_blocks):
        wrap = n + d >= n_blocks
        m = jnp.where(wrap, n + d - n_blocks, n + d)
        sm = pltpu.roll(s, n_blocks - d, 0)
        beats = (sm > s) | ((sm == s) & wrap)
        rank = rank + jnp.where(beats & (m < n_past), 1.0, 0.0)
    sel = (n < n_past) & (rank < float(MOBA_TOPK))
    return jnp.where(sel, 0.0, MASKED)


def _moba_prompt_kernel(q_ref, k_ref, v_ref, kmean_ref, o_ref,
                        q2_ref, neg_ref, m_ref, l_ref, acc_ref):
    qi = pl.program_id(1)
    per_block = MOBA_BLOCK // Q_TILE
    own = qi // per_block
    rows = HEAD_PAIR * Q_TILE
    n_blocks = kmean_ref.shape[1]
    pairs = range(N_PAIRS)
    lanes = [slice(p * LANES, (p + 1) * LANES) for p in pairs]
    r = lax.broadcasted_iota(jnp.int32, (rows, MOBA_BLOCK), 0)
    c = lax.broadcasted_iota(jnp.int32, (rows, MOBA_BLOCK), 1)
    own_mask = c <= r % Q_TILE + (qi % per_block) * Q_TILE
    kmean = kmean_ref[0]

    neg_t = []
    for p in pairs:
        qf = q_ref[:, lanes[p]]
        q2 = _stack_pair(qf, jnp.zeros_like(qf))
        q2_ref[p] = (q2 * Q_SCALE).astype(jnp.bfloat16)
        qh, ql = _split2(q2)
        kh, kl = _split2(kmean[:, lanes[p]])
        scores_t = _dot_nt(kh, qh) + _dot_nt(kl, qh) + _dot_nt(kh, ql)
        neg_t.append(_select_bias_rows(scores_t, own))
    neg_t.append(jnp.zeros((LANES - N_PAIRS * n_blocks, rows), jnp.float32))
    neg_ref[...] = jnp.concatenate(neg_t, axis=0).T.astype(jnp.bfloat16)
    ones = jnp.ones((MOBA_BLOCK, LANES), jnp.bfloat16)
    sub16 = lax.broadcasted_iota(jnp.int32, (16, LANES), 1)

    def block(n, is_own):
        for g in range(0, N_PAIRS, MOBA_PAIR_GROUP):
            stages(n, is_own, range(g, g + MOBA_PAIR_GROUP))

    def stages(n, is_own, pairs):
        start = pl.multiple_of(n * MOBA_BLOCK, MOBA_BLOCK)
        kp = {p: k_ref[pl.ds(start, MOBA_BLOCK), lanes[p]] for p in pairs}
        vp = {p: jnp.concatenate([v_ref[pl.ds(start, MOBA_BLOCK), lanes[p]], ones], axis=1)
              for p in pairs}
        if is_own:
            logits = {p: jnp.where(own_mask, _dot_nt(q2_ref[p], kp[p]), -jnp.inf) for p in pairs}
            m_new = {p: jnp.broadcast_to(jnp.max(logits[p], axis=1, keepdims=True), (rows, LANES))
                     for p in pairs}
        else:
            neg = neg_ref[...]
            pick = {p: jnp.concatenate(
                [jnp.where(sub16 == p * n_blocks + n, 1.0, 0.0).astype(jnp.bfloat16)]
                * (MOBA_BLOCK // 16), axis=0) for p in pairs}
            logits = {p: _dot_nt(jnp.concatenate([q2_ref[p], neg], axis=1),
                                 jnp.concatenate([kp[p], pick[p]], axis=1)) for p in pairs}
            m_old = {p: m_ref[p] for p in pairs}
            m_new = {p: jnp.maximum(m_old[p], jnp.max(logits[p], axis=1, keepdims=True))
                     for p in pairs}
        pr = {p: jnp.exp(logits[p] - jnp.concatenate([m_new[p]] * (MOBA_BLOCK // LANES), axis=1))
              for p in pairs}
        pv = {p: _dot(pr[p].astype(jnp.bfloat16), vp[p]) for p in pairs}
        for p in pairs:
            if is_own:
                l_ref[p] = pv[p][:, LANES:]
                acc_ref[p] = pv[p][:, :LANES]
            else:
                alpha = jnp.exp(m_old[p] - m_new[p])
                l_ref[p] = alpha * l_ref[p] + pv[p][:, LANES:]
                acc_ref[p] = alpha * acc_ref[p] + pv[p][:, :LANES]
            m_ref[p] = m_new[p]

    block(own, True)

    def body(n, carry):
        block(n, False)
        return carry

    lax.fori_loop(0, own, body, 0)
    for p in range(N_PAIRS):
        o_ref[:, p * LANES:(p + 1) * LANES] = _unstack_pair(
            acc_ref[p] / l_ref[p]).astype(o_ref.dtype)


def _moba_prompt(qf, k16, v16, kmean, batch, seq):
    nq = seq // Q_TILE
    nb = seq // MOBA_BLOCK
    qspec = pl.BlockSpec((Q_TILE, WIDTH), lambda b, i: (b * nq + i, 0))
    kvspec = pl.BlockSpec((seq, WIDTH), lambda b, i: (b, 0))
    stacked = (N_PAIRS, HEAD_PAIR * Q_TILE, LANES)
    return pl.pallas_call(
        _moba_prompt_kernel,
        grid=(batch, nq),
        in_specs=[qspec, kvspec, kvspec,
                  pl.BlockSpec((1, nb, WIDTH), lambda b, i: (b, 0, 0))],
        out_specs=qspec,
        out_shape=jax.ShapeDtypeStruct((batch * seq, WIDTH), jnp.bfloat16),
        scratch_shapes=[pltpu.VMEM(stacked, jnp.bfloat16),
                        pltpu.VMEM(stacked[1:], jnp.bfloat16)]
        + [pltpu.VMEM(stacked, jnp.float32)] * 3,
        compiler_params=pltpu.CompilerParams(
            dimension_semantics=("arbitrary", "arbitrary"), vmem_limit_bytes=VMEM_LIMIT),
        name="moba_prompt",
    )(qf, k16, v16, kmean)


def _decode_kernel(pt_ref, q_ref, knew_ref, vnew_ref, *refs, moba, n_pages, n_new):
    del pt_ref
    k_pages, v_pages, o_ref = refs[:n_pages], refs[n_pages:2 * n_pages], refs[2 * n_pages]
    rows = n_new * N_HEADS
    q = q_ref[0]
    sub = lax.broadcasted_iota(jnp.int32, (N_HEADS, WIDTH), 0)
    ln = lax.broadcasted_iota(jnp.int32, (N_HEADS, WIDTH), 1)
    head_mask = (ln // HEAD_DIM) == sub
    qbd = jnp.concatenate(
        [jnp.where(head_mask, jnp.broadcast_to(q[j:j + 1, :], (N_HEADS, WIDTH)), 0.0)
         for j in range(n_new)], axis=0)
    r = lax.broadcasted_iota(jnp.int32, (rows, PAGE_SIZE), 0)
    c = lax.broadcasted_iota(jnp.int32, (rows, PAGE_SIZE), 1)
    zpad = jnp.zeros((PAGE_SIZE - n_new, WIDTH), jnp.float32)
    k_new = jnp.concatenate([knew_ref[0], zpad], axis=0).astype(jnp.bfloat16)
    v_new = jnp.concatenate([vnew_ref[0], zpad], axis=0).astype(jnp.bfloat16)
    page16 = lambda ref: ref[0].astype(jnp.bfloat16)
    order = list(reversed(range(n_pages)))

    if moba:
        qb = (qbd * Q_SCALE).astype(jnp.bfloat16)
        per_block = MOBA_BLOCK // PAGE_SIZE
        n_blocks = n_pages // per_block
        lane = lax.broadcasted_iota(jnp.int32, (WIDTH, LANES), 1)
        kmean_t = jnp.zeros((WIDTH, LANES), jnp.float32)
        for n in range(n_blocks):
            block_sum = sum(k_pages[n * per_block + i][0] for i in range(per_block))
            col = jnp.sum(block_sum, axis=1, keepdims=True) * (1.0 / MOBA_BLOCK)
            kmean_t = jnp.where(lane == n, col, kmean_t)
        qh, ql = _split2(qbd)
        kh, kl = _split2(kmean_t)
        scores = _dot(qh, kh) + _dot(qh, kl) + _dot(ql, kh)
        bias = _select_bias(scores, n_blocks, n_blocks)
        logits = [jnp.where(c <= r // N_HEADS, _dot_nt(qb, k_new), -jnp.inf)]
        logits += [_dot(qb, page16(k_pages[p])) + bias[:, p // per_block:p // per_block + 1]
                   for p in order]
        top = functools.reduce(jnp.maximum, logits)
        m = jnp.max(top, axis=1, keepdims=True)
        pr = [jnp.exp(lg - m) for lg in logits]
        l = jnp.sum(functools.reduce(jnp.add, pr), axis=1, keepdims=True)
        acc = _dot(pr[0].astype(jnp.bfloat16), v_new)
        for i, p in enumerate(order):
            acc = acc + _dot_nt(pr[i + 1].astype(jnp.bfloat16), page16(v_pages[p]))
        acc = acc / l
    else:
        qb = qbd.astype(jnp.bfloat16)
        umat = _suffix_matrix()
        new_mask = c < r // N_HEADS
        z = [_dot_nt(qb, k_new)] + [_dot(qb, page16(k_pages[p])) for p in order]
        ls = [_log_sigmoid(zi) for zi in z]
        lk = [lsi - zi for lsi, zi in zip(ls, z)]
        lk[0] = jnp.where(new_mask, lk[0], 0.0)
        s = _suffix_sums(jnp.concatenate(lk, axis=0), umat)
        a, run = [], None
        for i in range(len(z)):
            si = s[i * rows:(i + 1) * rows]
            e = ls[i] + si[:, :PAGE_SIZE]
            a.append(jnp.exp(e if run is None else e + run))
            run = si[:, PAGE_SIZE:] if run is None else run + si[:, PAGE_SIZE:]
        a[0] = jnp.where(new_mask, a[0], 0.0)
        acc = _dot(a[0].astype(jnp.bfloat16), v_new)
        for i, p in enumerate(order):
            acc = acc + _dot_nt(a[i + 1].astype(jnp.bfloat16), page16(v_pages[p]))

    o_ref[0] = jnp.concatenate(
        [jnp.sum(jnp.where(head_mask, acc[j * N_HEADS:(j + 1) * N_HEADS, :], 0.0),
                 axis=0, keepdims=True) for j in range(n_new)], axis=0)


def _decode(page_table, q, k_new, v_new, cache_k, cache_v, moba):
    b, n_new, _ = q.shape
    n_pages = page_table.shape[1]
    tok_spec = pl.BlockSpec((1, n_new, WIDTH), lambda i, pt: (i, 0, 0))

    def page_spec(p):
        return pl.BlockSpec((1, WIDTH, PAGE_SIZE), lambda i, pt: (pt[i, p], 0, 0))

    grid_spec = pltpu.PrefetchScalarGridSpec(
        num_scalar_prefetch=1,
        grid=(b,),
        in_specs=[tok_spec, tok_spec, tok_spec]
        + [page_spec(p) for p in range(n_pages)] * 2,
        out_specs=tok_spec,
    )
    return pl.pallas_call(
        functools.partial(_decode_kernel, moba=moba, n_pages=n_pages, n_new=n_new),
        grid_spec=grid_spec,
        out_shape=jax.ShapeDtypeStruct((b, n_new, WIDTH), jnp.float32),
        compiler_params=pltpu.CompilerParams(
            dimension_semantics=("arbitrary",), vmem_limit_bytes=VMEM_LIMIT),
        name="moba_decode" if moba else "sb_decode",
    )(page_table, q, k_new, v_new, *([cache_k] * n_pages), *([cache_v] * n_pages))


def _route(lg):
    lane = lax.broadcasted_iota(jnp.int32, lg.shape, 1)
    lane_f = lane.astype(jnp.float32)
    far = float(LANES)
    is_group = lane < N_GROUPS
    gl = jnp.where(is_group, lg, -jnp.inf)
    gmax = jnp.max(gl, axis=1, keepdims=True)
    gsum = jnp.sum(jnp.where(is_group, jnp.exp(gl - gmax), 0.0), axis=1, keepdims=True)
    g_p = 1.0 / gsum
    g_i = jnp.min(jnp.where(gl == gmax, lane_f, far), axis=1, keepdims=True)
    lo = N_GROUPS + EXPERTS_PER_GROUP * g_i
    in_group = (lane_f >= lo) & (lane_f < lo + EXPERTS_PER_GROUP)
    el = jnp.where(in_group, lg, -jnp.inf)
    m1 = jnp.max(el, axis=1, keepdims=True)
    i1 = jnp.min(jnp.where(el == m1, lane_f, far), axis=1, keepdims=True)
    el2 = jnp.where(lane_f == i1, -jnp.inf, el)
    m2 = jnp.max(el2, axis=1, keepdims=True)
    i2 = jnp.min(jnp.where(el2 == m2, lane_f, far), axis=1, keepdims=True)
    t = jnp.exp(m2 - m1)
    w1 = g_p / (1.0 + t)
    w2 = w1 * t
    return jnp.where(lane_f == i1, w1, 0.0) + jnp.where(lane_f == i2, w2, 0.0)


def _merge_kernel(x_ref, osb_ref, om_ref, ga_ref, gb_ref, wps_ref, wpm_ref, wout_ref,
                  g_ref, wrh_ref, wrl_ref, br_ref, hp_ref, xn_ref, comb_ref):
    tm = x_ref.shape[0]
    rc = min(tm, MERGE_ROW_CHUNK)
    chunks = [slice(c * rc, (c + 1) * rc) for c in range(tm // rc)]

    def branches(rows):
        return (_dot(osb_ref[rows, :].astype(jnp.bfloat16), wps_ref[...]),
                _dot(om_ref[rows, :].astype(jnp.bfloat16), wpm_ref[...]))

    def residual(rows, b):
        m = jax.nn.sigmoid(ga_ref[rows, :]) * b[0] + jax.nn.sigmoid(gb_ref[rows, :]) * b[1]
        hp = x_ref[rows, :] + _dot(m.astype(jnp.bfloat16), wout_ref[...])
        hp_ref[rows, :] = hp
        return hp

    def route(rows, hp):
        xn = _rms_norm(hp, g_ref[...])
        xn_ref[rows, :] = xn.astype(jnp.bfloat16)
        xh, xl = _split2(xn)
        wh, wl = wrh_ref[...], wrl_ref[...]
        lg = _dot(xh, wh) + _dot(xh, wl) + _dot(xl, wh) + br_ref[...]
        comb_ref[rows, :] = _route(lg)

    b = [branches(rows) for rows in chunks]
    hp = [residual(rows, b[i]) for i, rows in enumerate(chunks)]
    for i, rows in enumerate(chunks):
        route(rows, hp[i])


def _merge(x, o_sb, o_m, ga, gb, wps, wpm, wout, g_ffn, wr_hi, wr_lo, br, tm):
    t = x.shape[0]
    row = lambda i: (i, 0)
    fixed = lambda i: (0, 0)
    return pl.pallas_call(
        _merge_kernel,
        grid=(t // tm,),
        in_specs=[pl.BlockSpec((tm, D_MODEL), row),
                  pl.BlockSpec((tm, WIDTH), row),
                  pl.BlockSpec((tm, WIDTH), row),
                  pl.BlockSpec((tm, D_MODEL), row),
                  pl.BlockSpec((tm, D_MODEL), row),
                  pl.BlockSpec((WIDTH, D_MODEL), fixed),
                  pl.BlockSpec((WIDTH, D_MODEL), fixed),
                  pl.BlockSpec((D_MODEL, D_MODEL), fixed),
                  pl.BlockSpec((1, D_MODEL), fixed),
                  pl.BlockSpec((D_MODEL, LANES), fixed),
                  pl.BlockSpec((D_MODEL, LANES), fixed),
                  pl.BlockSpec((1, LANES), fixed)],
        out_specs=(pl.BlockSpec((tm, D_MODEL), row),
                   pl.BlockSpec((tm, D_MODEL), row),
                   pl.BlockSpec((tm, LANES), row)),
        out_shape=(jax.ShapeDtypeStruct((t, D_MODEL), jnp.float32),
                   jax.ShapeDtypeStruct((t, D_MODEL), jnp.bfloat16),
                   jax.ShapeDtypeStruct((t, LANES), jnp.float32)),
        compiler_params=pltpu.CompilerParams(
            dimension_semantics=("arbitrary",), vmem_limit_bytes=VMEM_LIMIT),
        name="merge",
    )(x, o_sb, o_m, ga, gb, wps, wpm, wout, g_ffn, wr_hi, wr_lo, br)


def _moe_kernel(xn_ref, comb_ref, hp_ref, wg_ref, wu_ref, wd_ref, gf_ref, y_ref):
    tm = xn_ref.shape[0]
    rc = min(tm, MOE_ROW_CHUNK)
    n_chunks = tm // rc
    y_ref[...] = hp_ref[...]
    lane = lax.broadcasted_iota(jnp.int32, (rc, LANES), 1)

    def gate_up(e, c):
        xc = xn_ref[c * rc:(c + 1) * rc, :]
        return _dot(xc, wg_ref[e]), _dot(xc, wu_ref[e])

    def expert(e, nxt):
        for c in range(n_chunks):
            a, u = nxt
            if c + 1 < n_chunks:
                nxt = gate_up(e, c + 1)
            else:
                nxt = gate_up(jnp.minimum(e + 1, N_EXPERTS - 1), 0)
            rows = slice(c * rc, (c + 1) * rc)
            w = jnp.sum(jnp.where(lane == e + N_GROUPS, comb_ref[rows, :], 0.0),
                        axis=1, keepdims=True)
            hdn = (a * jax.nn.sigmoid(a)) * u * w
            y_ref[rows, :] += _dot(hdn.astype(jnp.bfloat16), wd_ref[e])
        return nxt

    lax.fori_loop(0, N_EXPERTS, expert, gate_up(0, 0))
    y_ref[...] = _rms_norm(y_ref[...], gf_ref[...])


def _moe(xn16, comb, hp, wg, wu, wd, g_final, tm):
    t = xn16.shape[0]
    row = lambda i: (i, 0)
    resident = lambda shape: pl.BlockSpec(shape, lambda i: (0,) * len(shape),
                                          pipeline_mode=pl.Buffered(1))
    return pl.pallas_call(
        _moe_kernel,
        grid=(t // tm,),
        in_specs=[pl.BlockSpec((tm, D_MODEL), row),
                  pl.BlockSpec((tm, LANES), row),
                  pl.BlockSpec((tm, D_MODEL), row),
                  resident((N_EXPERTS, D_MODEL, D_EXPERT)),
                  resident((N_EXPERTS, D_MODEL, D_EXPERT)),
                  resident((N_EXPERTS, D_EXPERT, D_MODEL)),
                  resident((1, D_MODEL))],
        out_specs=pl.BlockSpec((tm, D_MODEL), row),
        out_shape=jax.ShapeDtypeStruct((t, D_MODEL), jnp.float32),
        compiler_params=pltpu.CompilerParams(
            dimension_semantics=("arbitrary",), vmem_limit_bytes=VMEM_LIMIT),
        name="moe",
    )(xn16, comb, hp, wg, wu, wd, g_final)


def _tail(x, o_sb, o_m, ga, gb, lw, g_out, tm_merge, tm_moe):
    hp, xn16, comb = _merge(x, o_sb, o_m, ga, gb, lw["wps"], lw["wpm"], lw["wout"],
                            lw["g_ffn"], lw["wr_hi"], lw["wr_lo"], lw["br"], tm_merge)
    return _moe(xn16, comb, hp, lw["wg"], lw["wu"], lw["wd"], g_out, tm_moe)


def kernel(x_prompt, x_sample, cache_sb_k, cache_sb_v, cache_moba_k, cache_moba_v, page_table,
           g_mix, w_in, w_proj_sb, w_proj_moba, w_out, g_ffn, w_router_group, b_router_group,
           w_router_expert, b_router_expert, w_expert_gate, w_expert_up, w_expert_down, g_final):
    batch, seq, _ = x_prompt.shape
    dec_b, dec_seq, _ = x_sample.shape
    depth = w_in.shape[0]
    assert depth == 1, "the final RMSNorm is fused into the last layer's MoE kernel"
    n_phys = cache_sb_k.shape[1]
    past_len = page_table.shape[1] * PAGE_SIZE
    bf16 = jnp.bfloat16
    tm_p = 256
    tm_s = min(256, dec_b * dec_seq)

    tab_p = _rope_tables(jnp.arange(tm_p * (seq // tm_p)))
    tab_s = _rope_tables(past_len + (jnp.arange(tm_s) % dec_seq))

    hp = x_prompt.reshape(batch * seq, D_MODEL)
    hs = x_sample.reshape(dec_b * dec_seq, D_MODEL)
    outs = [[] for _ in range(8)]
    for layer in range(depth):
        w_r = jnp.concatenate([w_router_group[layer], w_router_expert[layer]], axis=1)
        w_r = jnp.pad(w_r, ((0, 0), (0, LANES - w_r.shape[1])))
        wr_hi = w_r.astype(bf16)
        b_r = jnp.concatenate([b_router_group[layer], b_router_expert[layer]])
        lw = dict(
            wps=w_proj_sb[layer].astype(bf16), wpm=w_proj_moba[layer].astype(bf16),
            wout=w_out[layer].astype(bf16), g_ffn=g_ffn[layer][None, :],
            wr_hi=wr_hi, wr_lo=(w_r - wr_hi.astype(jnp.float32)).astype(bf16),
            br=jnp.pad(b_r, (0, LANES - b_r.shape[0]))[None, :],
            wg=w_expert_gate[layer].astype(bf16), wu=w_expert_up[layer].astype(bf16),
            wd=w_expert_down[layer].astype(bf16))
        w16 = w_in[layer].astype(bf16)
        g = g_mix[layer][None, :]
        g_out = g_final[None, :]

        (q_sb, k_sb, v_sb, q_m, k_m, v_m, ga, gb,
         k_sb16, v_sb16, k_m16, v_m16, kmean) = _inproj(hp, g, w16, tab_p, tm_p, seq)
        o_sb = _sb_prompt(q_sb, k_sb16, v_sb16, batch, seq)
        o_m = _moba_prompt(q_m, k_m16, v_m16,
                           kmean.reshape(batch, seq // MOBA_BLOCK, WIDTH), batch, seq)
        hp = _tail(hp, o_sb, o_m, ga, gb, lw, g_out, 512, 1024)
        for dst, val in zip(outs[:4], (k_sb, v_sb, k_m, v_m)):
            dst.append(jnp.transpose(val.reshape(batch, N_HEADS, HEAD_DIM, seq), (0, 3, 1, 2)))

        (q_sb, k_sb, v_sb, q_m, k_m, v_m, ga, gb, *_) = _inproj(hs, g, w16, tab_s, tm_s)
        tok = lambda a: a.astype(jnp.float32).reshape(dec_b, dec_seq, WIDTH)
        pages = lambda cch: jnp.transpose(cch[layer], (0, 2, 3, 1)).reshape(
            n_phys, WIDTH, PAGE_SIZE)
        o_sb = _decode(page_table, tok(q_sb), tok(k_sb), tok(v_sb),
                       pages(cache_sb_k), pages(cache_sb_v), moba=False)
        o_m = _decode(page_table, tok(q_m), tok(k_m), tok(v_m),
                      pages(cache_moba_k), pages(cache_moba_v), moba=True)
        flat = lambda a: a.reshape(dec_b * dec_seq, WIDTH)
        hs = _tail(hs, flat(o_sb), flat(o_m), ga, gb, lw, g_out, tm_s * 2, tm_s * 2)
        for dst, val in zip(outs[4:], (k_sb, v_sb, k_m, v_m)):
            dst.append(val.reshape(dec_b, dec_seq, N_HEADS, HEAD_DIM))

    y_prompt = hp.reshape(batch, seq, D_MODEL)
    y_sample = hs.reshape(dec_b, dec_seq, D_MODEL)
    return (y_prompt, y_sample) + tuple(jnp.stack(o) for o in outs)
```

```python
import functools

import jax
import jax.numpy as jnp
from jax import lax
from jax.experimental import pallas as pl
from jax.experimental.pallas import tpu as pltpu

D_MODEL = 1024
HEAD_DIM = 64
N_HEADS = 8
WIDTH = N_HEADS * HEAD_DIM
IN_WIDTH = 6 * WIDTH + 2 * D_MODEL
ROPE_DIM = HEAD_DIM // 4
ROPE_HALF = ROPE_DIM // 2
ROPE_THETA = 500000.0
PAGE_SIZE = 128
MOBA_BLOCK = 256
MOBA_TOPK = 3
N_GROUPS = 4
EXPERTS_PER_GROUP = 4
N_EXPERTS = N_GROUPS * EXPERTS_PER_GROUP
D_EXPERT = 256
NORM_EPS = 1e-6
Q_SCALE = HEAD_DIM ** -0.5

LANES = 128
HEAD_PAIR = LANES // HEAD_DIM
N_PAIRS = N_HEADS // HEAD_PAIR
ATT_BLOCK = 128
Q_TILE = 256
MOE_ROW_CHUNK = 256
MOBA_PAIR_GROUP = 2
MERGE_ROW_CHUNK = 256
DECODE_ROWS = 2
VMEM_LIMIT = 56 * 1024 * 1024

MASKED = -1e30
EXP_UNDERFLOW = -104.0

_NT = (((1,), (1,)), ((), ()))


def _dot(a, b):
    return jnp.dot(a, b, preferred_element_type=jnp.float32)


def _dot_nt(a, b):
    return lax.dot_general(a, b, _NT, preferred_element_type=jnp.float32)


def _split2(x):
    hi = x.astype(jnp.bfloat16)
    lo = (x - hi.astype(jnp.float32)).astype(jnp.bfloat16)
    return hi, lo


def _log_sigmoid(z):
    return jnp.minimum(z, 0.0) - jnp.log(1.0 + jnp.exp(-jnp.abs(z)))


def _suffix_matrix():
    r = lax.broadcasted_iota(jnp.int32, (2 * ATT_BLOCK, 2 * ATT_BLOCK), 0) % ATT_BLOCK
    c = lax.broadcasted_iota(jnp.int32, (2 * ATT_BLOCK, 2 * ATT_BLOCK), 1)
    return jnp.where((c >= ATT_BLOCK) | (r > c), 1.0, 0.0).astype(jnp.bfloat16)


def _suffix_sums(lk, umat):
    return _dot(jnp.concatenate(_split2(lk), axis=1), umat)


def _rms_norm(x, g):
    r = lax.rsqrt(jnp.mean(x * x, axis=-1, keepdims=True) + NORM_EPS)
    return (x * r) * g


def _rope(h, ca, cb, cc):
    outs = []
    for g in range(WIDTH // LANES):
        xg = h[:, g * LANES:(g + 1) * LANES]
        outs.append(xg * ca + pltpu.roll(xg, LANES - ROPE_HALF, 1) * cb
                    + pltpu.roll(xg, ROPE_HALF, 1) * cc)
    return jnp.concatenate(outs, axis=1)


def _inproj_kernel(x_ref, g_ref, w_ref, ca_ref, cb_ref, cc_ref,
                   qsb_ref, ksb_ref, vsb_ref, qm_ref, km_ref, vm_ref, ga_ref, gb_ref,
                   ksb16_ref, vsb16_ref, km16_ref, vm16_ref, kmean_ref, *, transposed):
    xn = _rms_norm(x_ref[...], g_ref[...]).astype(jnp.bfloat16)
    ca, cb, cc = ca_ref[...], cb_ref[...], cc_ref[...]

    def proj(c, width=WIDTH):
        return _dot(xn, w_ref[:, c:c + width])

    def emit(ref, ref16, val):
        if transposed:
            ref[0] = val.T
        else:
            ref[...] = val
        ref16[...] = val.astype(jnp.bfloat16)

    qsb_ref[...] = (proj(0) * Q_SCALE).astype(jnp.bfloat16)
    emit(ksb_ref, ksb16_ref, proj(WIDTH))
    emit(vsb_ref, vsb16_ref, proj(2 * WIDTH))
    qm_ref[...] = _rope(proj(3 * WIDTH), ca, cb, cc)
    k = _rope(proj(4 * WIDTH), ca, cb, cc)
    emit(km_ref, km16_ref, k)
    kmean_ref[0] = jnp.sum(k, axis=0, keepdims=True) * (1.0 / MOBA_BLOCK)
    emit(vm_ref, vm16_ref, proj(5 * WIDTH))
    for half in range(2):
        ga_ref[:, half * WIDTH:(half + 1) * WIDTH] = proj(6 * WIDTH + half * WIDTH)
        gb_ref[:, half * WIDTH:(half + 1) * WIDTH] = proj(6 * WIDTH + D_MODEL + half * WIDTH)


def _rope_tables(pos):
    inv = jnp.float32(ROPE_THETA) ** (-jnp.arange(ROPE_HALF, dtype=jnp.float32) / ROPE_HALF)
    ang = pos.astype(jnp.float32)[:, None] * inv[None, :]
    cos, sin = jnp.cos(ang), jnp.sin(ang)
    n = pos.shape[0]
    rest = HEAD_DIM - ROPE_DIM
    one = jnp.ones((n, rest), jnp.float32)
    zero = jnp.zeros((n, rest), jnp.float32)
    zh = jnp.zeros((n, ROPE_HALF), jnp.float32)
    a = jnp.concatenate([cos, cos, one], axis=1)
    b = jnp.concatenate([-sin, zh, zero], axis=1)
    c = jnp.concatenate([zh, sin, zero], axis=1)
    rep = LANES // HEAD_DIM
    return tuple(jnp.tile(t, (1, rep)) for t in (a, b, c))


def _inproj(x, g, w16, tables, tm, seq=None):
    t = x.shape[0]
    nt = tables[0].shape[0] // tm
    row = lambda i: (i, 0)
    fixed = lambda i: (0, 0)
    f32, bf16 = jnp.float32, jnp.bfloat16
    slab = lambda dt, w=WIDTH: jax.ShapeDtypeStruct((t, w), dt)
    if seq is None:
        kv_shape, kv_spec = slab(f32), pl.BlockSpec((tm, WIDTH), row)
    else:
        assert seq == nt * tm
        kv_shape = jax.ShapeDtypeStruct((t // seq, WIDTH, seq), f32)
        kv_spec = pl.BlockSpec((1, WIDTH, tm), lambda i: (i // nt, 0, i % nt))
    out_shape = (slab(bf16), kv_shape, kv_shape, slab(f32), kv_shape, kv_shape,
                 slab(f32, D_MODEL), slab(f32, D_MODEL),
                 slab(bf16), slab(bf16), slab(bf16), slab(bf16),
                 jax.ShapeDtypeStruct((t // tm, 1, WIDTH), f32))
    tok_spec = pl.BlockSpec((tm, WIDTH), row)
    out_specs = tuple([tok_spec, kv_spec, kv_spec, tok_spec, kv_spec, kv_spec]
                      + [pl.BlockSpec((tm, D_MODEL), row)] * 2
                      + [tok_spec] * 4
                      + [pl.BlockSpec((1, 1, WIDTH), lambda i: (i, 0, 0))])
    tab_spec = pl.BlockSpec((tm, LANES), lambda i: (i % nt, 0))
    return pl.pallas_call(
        functools.partial(_inproj_kernel, transposed=seq is not None),
        grid=(t // tm,),
        in_specs=[pl.BlockSpec((tm, D_MODEL), row),
                  pl.BlockSpec((1, D_MODEL), fixed),
                  pl.BlockSpec((D_MODEL, IN_WIDTH), fixed),
                  tab_spec, tab_spec, tab_spec],
        out_specs=out_specs,
        out_shape=out_shape,
        compiler_params=pltpu.CompilerParams(
            dimension_semantics=("arbitrary",), vmem_limit_bytes=VMEM_LIMIT),
        name="inproj",
    )(x, g, w16, *tables)


def _stack_pair(x, zero):
    low = lax.broadcasted_iota(jnp.int32, x.shape, 1) < HEAD_DIM
    return jnp.concatenate([jnp.where(low, x, zero), jnp.where(low, zero, x)], axis=0)


def _unstack_pair(y):
    half = y.shape[0] // 2
    low = lax.broadcasted_iota(jnp.int32, (half, LANES), 1) < HEAD_DIM
    return jnp.where(low, y[:half], y[half:])


def _sb_prompt_kernel(q_ref, k_ref, v_ref, o_ref, q2_ref, run_ref, acc_ref):
    qi = pl.program_id(1)
    umat = _suffix_matrix()
    rows = HEAD_PAIR * Q_TILE
    r = lax.broadcasted_iota(jnp.int32, (rows, ATT_BLOCK), 0) % Q_TILE
    c = lax.broadcasted_iota(jnp.int32, (rows, ATT_BLOCK), 1)
    pairs = range(N_PAIRS)
    lanes = [slice(p * LANES, (p + 1) * LANES) for p in pairs]
    for p in pairs:
        qp = q_ref[:, lanes[p]]
        q2_ref[p] = _stack_pair(qp, jnp.zeros_like(qp))

    def tile(kb, first, masked):
        start = pl.multiple_of(kb * ATT_BLOCK, ATT_BLOCK)
        if masked:
            causal = kb * ATT_BLOCK + c < qi * Q_TILE + r
        z = [_dot_nt(q2_ref[p], k_ref[pl.ds(start, ATT_BLOCK), lanes[p]]) for p in pairs]
        ls = [_log_sigmoid(z[p]) for p in pairs]
        lk = [ls[p] - z[p] for p in pairs]
        if masked:
            lk = [jnp.where(causal, lk[p], 0.0) for p in pairs]
        s = [_suffix_sums(lk[p], umat) for p in pairs]
        e = [ls[p] + s[p][:, :ATT_BLOCK] for p in pairs]
        if not first:
            e = [e[p] + run_ref[p] for p in pairs]
        a = [jnp.exp(e[p]) for p in pairs]
        if masked:
            a = [jnp.where(causal, a[p], 0.0) for p in pairs]
        pv = [_unstack_pair(_dot(a[p].astype(jnp.bfloat16),
                                 v_ref[pl.ds(start, ATT_BLOCK), lanes[p]])) for p in pairs]
        run = [s[p][:, ATT_BLOCK:] if first else run_ref[p] + s[p][:, ATT_BLOCK:] for p in pairs]
        for p in pairs:
            run_ref[p] = run[p]
            if first:
                acc_ref[p] = pv[p]
            else:
                acc_ref[p] += pv[p]
        return jnp.max(functools.reduce(jnp.maximum, run))

    n_diag = Q_TILE // ATT_BLOCK
    last = (qi + 1) * n_diag - 1
    reach = tile(last, True, True)
    for j in range(1, n_diag):
        reach = tile(last - j, False, True)

    def more(carry):
        j, reach = carry
        return jnp.logical_and(j <= last, reach > EXP_UNDERFLOW)

    def body(carry):
        j, _ = carry
        return j + 1, tile(last - j, False, False)

    lax.while_loop(more, body, (n_diag, reach))
    for p in pairs:
        o_ref[:, lanes[p]] = acc_ref[p].astype(o_ref.dtype)


def _sb_prompt(q16, k16, v16, batch, seq):
    nq = seq // Q_TILE
    qspec = pl.BlockSpec((Q_TILE, WIDTH), lambda b, i: (b * nq + i, 0))
    kvspec = pl.BlockSpec((seq, WIDTH), lambda b, i: (b, 0))
    stacked = (N_PAIRS, HEAD_PAIR * Q_TILE, LANES)
    return pl.pallas_call(
        _sb_prompt_kernel,
        grid=(batch, nq),
        in_specs=[qspec, kvspec, kvspec],
        out_specs=qspec,
        out_shape=jax.ShapeDtypeStruct((batch * seq, WIDTH), jnp.bfloat16),
        scratch_shapes=[pltpu.VMEM(stacked, jnp.bfloat16),
                        pltpu.VMEM(stacked, jnp.float32),
                        pltpu.VMEM((N_PAIRS, Q_TILE, LANES), jnp.float32)],
        compiler_params=pltpu.CompilerParams(
            dimension_semantics=("arbitrary", "arbitrary"), vmem_limit_bytes=VMEM_LIMIT),
        name="sb_prompt",
    )(q16, k16, v16)


def _block_scores(qf, kmean_rows):
    pad = jnp.zeros((LANES - kmean_rows.shape[0], kmean_rows.shape[1]), jnp.float32)
    km = jnp.concatenate([kmean_rows, pad], axis=0)
    qh, ql = _split2(qf)
    kh, kl = _split2(km)
    return _dot_nt(qh, kh) + _dot_nt(qh, kl) + _dot_nt(ql, kh)


def _select_bias(s, n_past, n_blocks):
    lane = lax.broadcasted_iota(jnp.int32, s.shape, 1)
    rank = jnp.zeros(s.shape, jnp.float32)
    for m in range(n_blocks):
        sm = s[:, m:m + 1]
        beats = (sm > s) | ((sm == s) & (m < lane))
        rank = rank + jnp.where(beats, 1.0, 0.0) * jnp.where(m < n_past, 1.0, 0.0)
    sel = (lane < n_past) & (rank < float(MOBA_TOPK))
    return jnp.where(sel, 0.0, -jnp.inf)


def _select_bias_rows(s, n_past):
    n_blocks = s.shape[0]
    n = lax.broadcasted_iota(jnp.int32, s.shape, 0)
    rank = jnp.zeros(s.shape, jnp.float32)
    for d in range(1, n_blocks):
        wrap = n + d >= n_blocks
        m = jnp.where(wrap, n + d - n_blocks, n + d)
        sm = pltpu.roll(s, n_blocks - d, 0)
        beats = (sm > s) | ((sm == s) & wrap)
        rank = rank + jnp.where(beats & (m < n_past), 1.0, 0.0)
    sel = (n < n_past) & (rank < float(MOBA_TOPK))
    return jnp.where(sel, 0.0, MASKED)


def _moba_prompt_kernel(q_ref, k_ref, v_ref, kmean_ref, o_ref,
                        q2_ref, neg_ref, m_ref, l_ref, acc_ref):
    qi = pl.program_id(1)
    per_block = MOBA_BLOCK // Q_TILE
    own = qi // per_block
    rows = HEAD_PAIR * Q_TILE
    n_blocks = kmean_ref.shape[1]
    pairs = range(N_PAIRS)
    lanes = [slice(p * LANES, (p + 1) * LANES) for p in pairs]
    r = lax.broadcasted_iota(jnp.int32, (rows, MOBA_BLOCK), 0)
    c = lax.broadcasted_iota(jnp.int32, (rows, MOBA_BLOCK), 1)
    own_mask = c <= r % Q_TILE + (qi % per_block) * Q_TILE
    kmean = kmean_ref[0]

    neg_t = []
    for p in pairs:
        qf = q_ref[:, lanes[p]]
        q2 = _stack_pair(qf, jnp.zeros_like(qf))
        q2_ref[p] = (q2 * Q_SCALE).astype(jnp.bfloat16)
        qh, ql = _split2(q2)
        kh, kl = _split2(kmean[:, lanes[p]])
        scores_t = _dot_nt(kh, qh) + _dot_nt(kl, qh) + _dot_nt(kh, ql)
        neg_t.append(_select_bias_rows(scores_t, own))
    neg_t.append(jnp.zeros((LANES - N_PAIRS * n_blocks, rows), jnp.float32))
    neg_ref[...] = jnp.concatenate(neg_t, axis=0).T.astype(jnp.bfloat16)
    ones = jnp.ones((MOBA_BLOCK, LANES), jnp.bfloat16)
    sub16 = lax.broadcasted_iota(jnp.int32, (16, LANES), 1)

    def block(n, is_own):
        for g in range(0, N_PAIRS, MOBA_PAIR_GROUP):
            stages(n, is_own, range(g, g + MOBA_PAIR_GROUP))

    def stages(n, is_own, pairs):
        start = pl.multiple_of(n * MOBA_BLOCK, MOBA_BLOCK)
        kp = {p: k_ref[pl.ds(start, MOBA_BLOCK), lanes[p]] for p in pairs}
        vp = {p: jnp.concatenate([v_ref[pl.ds(start, MOBA_BLOCK), lanes[p]], ones], axis=1)
              for p in pairs}
        if is_own:
            logits = {p: jnp.where(own_mask, _dot_nt(q2_ref[p], kp[p]), -jnp.inf) for p in pairs}
            m_new = {p: jnp.broadcast_to(jnp.max(logits[p], axis=1, keepdims=True), (rows, LANES))
                     for p in pairs}
        else:
            neg = neg_ref[...]
            pick = {p: jnp.concatenate(
                [jnp.where(sub16 == p * n_blocks + n, 1.0, 0.0).astype(jnp.bfloat16)]
                * (MOBA_BLOCK // 16), axis=0) for p in pairs}
            logits = {p: _dot_nt(jnp.concatenate([q2_ref[p], neg], axis=1),
                                 jnp.concatenate([kp[p], pick[p]], axis=1)) for p in pairs}
            m_old = {p: m_ref[p] for p in pairs}
            m_new = {p: jnp.maximum(m_old[p], jnp.max(logits[p], axis=1, keepdims=True))
                     for p in pairs}
        pr = {p: jnp.exp(logits[p] - jnp.concatenate([m_new[p]] * (MOBA_BLOCK // LANES), axis=1))
              for p in pairs}
        pv = {p: _dot(pr[p].astype(jnp.bfloat16), vp[p]) for p in pairs}
        for p in pairs:
            if is_own:
                l_ref[p] = pv[p][:, LANES:]
                acc_ref[p] = pv[p][:, :LANES]
            else:
                alpha = jnp.exp(m_old[p] - m_new[p])
                l_ref[p] = alpha * l_ref[p] + pv[p][:, LANES:]
                acc_ref[p] = alpha * acc_ref[p] + pv[p][:, :LANES]
            m_ref[p] = m_new[p]

    block(own, True)

    def body(n, carry):
        block(n, False)
        return carry

    lax.fori_loop(0, own, body, 0)
    for p in range(N_PAIRS):
        o_ref[:, p * LANES:(p + 1) * LANES] = _unstack_pair(
            acc_ref[p] / l_ref[p]).astype(o_ref.dtype)


def _moba_prompt(qf, k16, v16, kmean, batch, seq):
    nq = seq // Q_TILE
    nb = seq // MOBA_BLOCK
    qspec = pl.BlockSpec((Q_TILE, WIDTH), lambda b, i: (b * nq + i, 0))
    kvspec = pl.BlockSpec((seq, WIDTH), lambda b, i: (b, 0))
    stacked = (N_PAIRS, HEAD_PAIR * Q_TILE, LANES)
    return pl.pallas_call(
        _moba_prompt_kernel,
        grid=(batch, nq),
        in_specs=[qspec, kvspec, kvspec,
                  pl.BlockSpec((1, nb, WIDTH), lambda b, i: (b, 0, 0))],
        out_specs=qspec,
        out_shape=jax.ShapeDtypeStruct((batch * seq, WIDTH), jnp.bfloat16),
        scratch_shapes=[pltpu.VMEM(stacked, jnp.bfloat16),
                        pltpu.VMEM(stacked[1:], jnp.bfloat16)]
        + [pltpu.VMEM(stacked, jnp.float32)] * 3,
        compiler_params=pltpu.CompilerParams(
            dimension_semantics=("arbitrary", "arbitrary"), vmem_limit_bytes=VMEM_LIMIT),
        name="moba_prompt",
    )(qf, k16, v16, kmean)


def _decode_kernel(pt_ref, q_ref, knew_ref, vnew_ref, *refs, moba, n_pages, n_new, n_rows):
    del pt_ref
    k_pages = [refs[i * n_pages:(i + 1) * n_pages] for i in range(n_rows)]
    v_pages = [refs[(n_rows + i) * n_pages:(n_rows + i + 1) * n_pages] for i in range(n_rows)]
    o_ref = refs[2 * n_rows * n_pages]
    rows = n_new * N_HEADS
    sub = lax.broadcasted_iota(jnp.int32, (N_HEADS, WIDTH), 0)
    ln = lax.broadcasted_iota(jnp.int32, (N_HEADS, WIDTH), 1)
    head_mask = (ln // HEAD_DIM) == sub
    r = lax.broadcasted_iota(jnp.int32, (rows, PAGE_SIZE), 0)
    c = lax.broadcasted_iota(jnp.int32, (rows, PAGE_SIZE), 1)
    zpad = jnp.zeros((PAGE_SIZE - n_new, WIDTH), jnp.float32)
    page16 = lambda ref: ref[0].astype(jnp.bfloat16)
    order = list(reversed(range(n_pages)))
    samples = range(n_rows)

    qbd, k_new, v_new = [], [], []
    for i in samples:
        q = q_ref[i]
        qbd.append(jnp.concatenate(
            [jnp.where(head_mask, jnp.broadcast_to(q[j:j + 1, :], (N_HEADS, WIDTH)), 0.0)
             for j in range(n_new)], axis=0))
        k_new.append(jnp.concatenate([knew_ref[i], zpad], axis=0).astype(jnp.bfloat16))
        v_new.append(jnp.concatenate([vnew_ref[i], zpad], axis=0).astype(jnp.bfloat16))

    def weighted_values(i, w):
        acc = _dot(w[0].astype(jnp.bfloat16), v_new[i])
        for j, p in enumerate(order):
            acc = acc + _dot_nt(w[j + 1].astype(jnp.bfloat16), page16(v_pages[i][p]))
        return acc

    if moba:
        per_block = MOBA_BLOCK // PAGE_SIZE
        n_blocks = n_pages // per_block
        lane = lax.broadcasted_iota(jnp.int32, (WIDTH, LANES), 1)
        qb = [(qbd[i] * Q_SCALE).astype(jnp.bfloat16) for i in samples]
        bias = []
        for i in samples:
            kmean_t = jnp.zeros((WIDTH, LANES), jnp.float32)
            for n in range(n_blocks):
                block_sum = sum(k_pages[i][n * per_block + j][0] for j in range(per_block))
                col = jnp.sum(block_sum, axis=1, keepdims=True) * (1.0 / MOBA_BLOCK)
                kmean_t = jnp.where(lane == n, col, kmean_t)
            qh, ql = _split2(qbd[i])
            kh, kl = _split2(kmean_t)
            bias.append(_select_bias(_dot(qh, kh) + _dot(qh, kl) + _dot(ql, kh),
                                     n_blocks, n_blocks))
        logits = [[jnp.where(c <= r // N_HEADS, _dot_nt(qb[i], k_new[i]), -jnp.inf)]
                  + [_dot(qb[i], page16(k_pages[i][p]))
                     + bias[i][:, p // per_block:p // per_block + 1] for p in order]
                  for i in samples]
        m = [jnp.max(functools.reduce(jnp.maximum, logits[i]), axis=1, keepdims=True)
             for i in samples]
        pr = [[jnp.exp(lg - m[i]) for lg in logits[i]] for i in samples]
        l = [jnp.sum(functools.reduce(jnp.add, pr[i]), axis=1, keepdims=True) for i in samples]
        acc = [weighted_values(i, pr[i]) / l[i] for i in samples]
    else:
        umat = _suffix_matrix()
        new_mask = c < r // N_HEADS
        qb = [qbd[i].astype(jnp.bfloat16) for i in samples]
        z = [[_dot_nt(qb[i], k_new[i])] + [_dot(qb[i], page16(k_pages[i][p])) for p in order]
             for i in samples]
        n_chunks = n_pages + 1
        ls = [[_log_sigmoid(zc) for zc in z[i]] for i in samples]
        lk = [[ls[i][j] - z[i][j] for j in range(n_chunks)] for i in samples]
        for i in samples:
            lk[i][0] = jnp.where(new_mask, lk[i][0], 0.0)
        s = _suffix_sums(jnp.concatenate([x for i in samples for x in lk[i]], axis=0), umat)
        acc = []
        for i in samples:
            a, run = [], None
            for j in range(n_chunks):
                sj = s[(i * n_chunks + j) * rows:(i * n_chunks + j + 1) * rows]
                e = ls[i][j] + sj[:, :PAGE_SIZE]
                a.append(jnp.exp(e if run is None else e + run))
                run = sj[:, PAGE_SIZE:] if run is None else run + sj[:, PAGE_SIZE:]
            a[0] = jnp.where(new_mask, a[0], 0.0)
            acc.append(weighted_values(i, a))

    for i in samples:
        o_ref[i] = jnp.concatenate(
            [jnp.sum(jnp.where(head_mask, acc[i][j * N_HEADS:(j + 1) * N_HEADS, :], 0.0),
                     axis=0, keepdims=True) for j in range(n_new)], axis=0)


def _decode(page_table, q, k_new, v_new, cache_k, cache_v, moba):
    b, n_new, _ = q.shape
    n_pages = page_table.shape[1]
    n_rows = DECODE_ROWS if b % DECODE_ROWS == 0 else 1
    tok_spec = pl.BlockSpec((n_rows, n_new, WIDTH), lambda i, pt: (i, 0, 0))

    def page_spec(row, p):
        return pl.BlockSpec((1, WIDTH, PAGE_SIZE),
                            lambda i, pt: (pt[i * n_rows + row, p], 0, 0))

    page_specs = [page_spec(row, p) for row in range(n_rows) for p in range(n_pages)]
    grid_spec = pltpu.PrefetchScalarGridSpec(
        num_scalar_prefetch=1,
        grid=(b // n_rows,),
        in_specs=[tok_spec, tok_spec, tok_spec] + page_specs * 2,
        out_specs=tok_spec,
    )
    n_ops = n_rows * n_pages
    return pl.pallas_call(
        functools.partial(_decode_kernel, moba=moba, n_pages=n_pages, n_new=n_new,
                          n_rows=n_rows),
        grid_spec=grid_spec,
        out_shape=jax.ShapeDtypeStruct((b, n_new, WIDTH), jnp.float32),
        compiler_params=pltpu.CompilerParams(
            dimension_semantics=("arbitrary",), vmem_limit_bytes=VMEM_LIMIT),
        name="moba_decode" if moba else "sb_decode",
    )(page_table, q, k_new, v_new, *([cache_k] * n_ops), *([cache_v] * n_ops))


def _route(lg):
    lane = lax.broadcasted_iota(jnp.int32, lg.shape, 1)
    lane_f = lane.astype(jnp.float32)
    far = float(LANES)
    is_group = lane < N_GROUPS
    gl = jnp.where(is_group, lg, -jnp.inf)
    gmax = jnp.max(gl, axis=1, keepdims=True)
    gsum = jnp.sum(jnp.where(is_group, jnp.exp(gl - gmax), 0.0), axis=1, keepdims=True)
    g_p = 1.0 / gsum
    g_i = jnp.min(jnp.where(gl == gmax, lane_f, far), axis=1, keepdims=True)
    lo = N_GROUPS + EXPERTS_PER_GROUP * g_i
    in_group = (lane_f >= lo) & (lane_f < lo + EXPERTS_PER_GROUP)
    el = jnp.where(in_group, lg, -jnp.inf)
    m1 = jnp.max(el, axis=1, keepdims=True)
    i1 = jnp.min(jnp.where(el == m1, lane_f, far), axis=1, keepdims=True)
    el2 = jnp.where(lane_f == i1, -jnp.inf, el)
    m2 = jnp.max(el2, axis=1, keepdims=True)
    i2 = jnp.min(jnp.where(el2 == m2, lane_f, far), axis=1, keepdims=True)
    t = jnp.exp(m2 - m1)
    w1 = g_p / (1.0 + t)
    w2 = w1 * t
    return jnp.where(lane_f == i1, w1, 0.0) + jnp.where(lane_f == i2, w2, 0.0)


def _merge_kernel(x_ref, osb_ref, om_ref, ga_ref, gb_ref, wps_ref, wpm_ref, wout_ref,
                  g_ref, wrh_ref, wrl_ref, br_ref, hp_ref, xn_ref, comb_ref):
    tm = x_ref.shape[0]
    rc = min(tm, MERGE_ROW_CHUNK)
    chunks = [slice(c * rc, (c + 1) * rc) for c in range(tm // rc)]

    def branches(rows):
        return (_dot(osb_ref[rows, :].astype(jnp.bfloat16), wps_ref[...]),
                _dot(om_ref[rows, :].astype(jnp.bfloat16), wpm_ref[...]))

    def residual(rows, b):
        m = jax.nn.sigmoid(ga_ref[rows, :]) * b[0] + jax.nn.sigmoid(gb_ref[rows, :]) * b[1]
        hp = x_ref[rows, :] + _dot(m.astype(jnp.bfloat16), wout_ref[...])
        hp_ref[rows, :] = hp
        return hp

    def route(rows, hp):
        xn = _rms_norm(hp, g_ref[...])
        xn_ref[rows, :] = xn.astype(jnp.bfloat16)
        xh, xl = _split2(xn)
        wh, wl = wrh_ref[...], wrl_ref[...]
        lg = _dot(xh, wh) + _dot(xh, wl) + _dot(xl, wh) + br_ref[...]
        comb_ref[rows, :] = _route(lg)

    b = [branches(rows) for rows in chunks]
    hp = [residual(rows, b[i]) for i, rows in enumerate(chunks)]
    for i, rows in enumerate(chunks):
        route(rows, hp[i])


def _merge(x, o_sb, o_m, ga, gb, wps, wpm, wout, g_ffn, wr_hi, wr_lo, br, tm):
    t = x.shape[0]
    row = lambda i: (i, 0)
    fixed = lambda i: (0, 0)
    return pl.pallas_call(
        _merge_kernel,
        grid=(t // tm,),
        in_specs=[pl.BlockSpec((tm, D_MODEL), row),
                  pl.BlockSpec((tm, WIDTH), row),
                  pl.BlockSpec((tm, WIDTH), row),
                  pl.BlockSpec((tm, D_MODEL), row),
                  pl.BlockSpec((tm, D_MODEL), row),
                  pl.BlockSpec((WIDTH, D_MODEL), fixed),
                  pl.BlockSpec((WIDTH, D_MODEL), fixed),
                  pl.BlockSpec((D_MODEL, D_MODEL), fixed),
                  pl.BlockSpec((1, D_MODEL), fixed),
                  pl.BlockSpec((D_MODEL, LANES), fixed),
                  pl.BlockSpec((D_MODEL, LANES), fixed),
                  pl.BlockSpec((1, LANES), fixed)],
        out_specs=(pl.BlockSpec((tm, D_MODEL), row),
                   pl.BlockSpec((tm, D_MODEL), row),
                   pl.BlockSpec((tm, LANES), row)),
        out_shape=(jax.ShapeDtypeStruct((t, D_MODEL), jnp.float32),
                   jax.ShapeDtypeStruct((t, D_MODEL), jnp.bfloat16),
                   jax.ShapeDtypeStruct((t, LANES), jnp.float32)),
        compiler_params=pltpu.CompilerParams(
            dimension_semantics=("arbitrary",), vmem_limit_bytes=VMEM_LIMIT),
        name="merge",
    )(x, o_sb, o_m, ga, gb, wps, wpm, wout, g_ffn, wr_hi, wr_lo, br)


def _moe_kernel(xn_ref, comb_ref, hp_ref, wg_ref, wu_ref, wd_ref, gf_ref, y_ref):
    tm = xn_ref.shape[0]
    rc = min(tm, MOE_ROW_CHUNK)
    n_chunks = tm // rc
    y_ref[...] = hp_ref[...]
    lane = lax.broadcasted_iota(jnp.int32, (rc, LANES), 1)

    def gate_up(e, c):
        xc = xn_ref[c * rc:(c + 1) * rc, :]
        return _dot(xc, wg_ref[e]), _dot(xc, wu_ref[e])

    def expert(e, nxt):
        for c in range(n_chunks):
            a, u = nxt
            if c + 1 < n_chunks:
                nxt = gate_up(e, c + 1)
            else:
                nxt = gate_up(jnp.minimum(e + 1, N_EXPERTS - 1), 0)
            rows = slice(c * rc, (c + 1) * rc)
            w = jnp.sum(jnp.where(lane == e + N_GROUPS, comb_ref[rows, :], 0.0),
                        axis=1, keepdims=True)
            hdn = (a * jax.nn.sigmoid(a)) * u * w
            y_ref[rows, :] += _dot(hdn.astype(jnp.bfloat16), wd_ref[e])
        return nxt

    lax.fori_loop(0, N_EXPERTS, expert, gate_up(0, 0))
    y_ref[...] = _rms_norm(y_ref[...], gf_ref[...])


def _moe(xn16, comb, hp, wg, wu, wd, g_final, tm):
    t = xn16.shape[0]
    row = lambda i: (i, 0)
    resident = lambda shape: pl.BlockSpec(shape, lambda i: (0,) * len(shape),
                                          pipeline_mode=pl.Buffered(1))
    return pl.pallas_call(
        _moe_kernel,
        grid=(t // tm,),
        in_specs=[pl.BlockSpec((tm, D_MODEL), row),
                  pl.BlockSpec((tm, LANES), row),
                  pl.BlockSpec((tm, D_MODEL), row),
                  resident((N_EXPERTS, D_MODEL, D_EXPERT)),
                  resident((N_EXPERTS, D_MODEL, D_EXPERT)),
                  resident((N_EXPERTS, D_EXPERT, D_MODEL)),
                  resident((1, D_MODEL))],
        out_specs=pl.BlockSpec((tm, D_MODEL), row),
        out_shape=jax.ShapeDtypeStruct((t, D_MODEL), jnp.float32),
        compiler_params=pltpu.CompilerParams(
            dimension_semantics=("arbitrary",), vmem_limit_bytes=VMEM_LIMIT),
        name="moe",
    )(xn16, comb, hp, wg, wu, wd, g_final)


def _tail(x, o_sb, o_m, ga, gb, lw, g_out, tm_merge, tm_moe):
    hp, xn16, comb = _merge(x, o_sb, o_m, ga, gb, lw["wps"], lw["wpm"], lw["wout"],
                            lw["g_ffn"], lw["wr_hi"], lw["wr_lo"], lw["br"], tm_merge)
    return _moe(xn16, comb, hp, lw["wg"], lw["wu"], lw["wd"], g_out, tm_moe)


def kernel(x_prompt, x_sample, cache_sb_k, cache_sb_v, cache_moba_k, cache_moba_v, page_table,
           g_mix, w_in, w_proj_sb, w_proj_moba, w_out, g_ffn, w_router_group, b_router_group,
           w_router_expert, b_router_expert, w_expert_gate, w_expert_up, w_expert_down, g_final):
    batch, seq, _ = x_prompt.shape
    dec_b, dec_seq, _ = x_sample.shape
    depth = w_in.shape[0]
    assert depth == 1, "the final RMSNorm is fused into the last layer's MoE kernel"
    n_phys = cache_sb_k.shape[1]
    past_len = page_table.shape[1] * PAGE_SIZE
    bf16 = jnp.bfloat16
    tm_p = 256
    tm_s = min(256, dec_b * dec_seq)

    tab_p = _rope_tables(jnp.arange(tm_p * (seq // tm_p)))
    tab_s = _rope_tables(past_len + (jnp.arange(tm_s) % dec_seq))

    hp = x_prompt.reshape(batch * seq, D_MODEL)
    hs = x_sample.reshape(dec_b * dec_seq, D_MODEL)
    outs = [[] for _ in range(8)]
    for layer in range(depth):
        w_r = jnp.concatenate([w_router_group[layer], w_router_expert[layer]], axis=1)
        w_r = jnp.pad(w_r, ((0, 0), (0, LANES - w_r.shape[1])))
        wr_hi = w_r.astype(bf16)
        b_r = jnp.concatenate([b_router_group[layer], b_router_expert[layer]])
        lw = dict(
            wps=w_proj_sb[layer].astype(bf16), wpm=w_proj_moba[layer].astype(bf16),
            wout=w_out[layer].astype(bf16), g_ffn=g_ffn[layer][None, :],
            wr_hi=wr_hi, wr_lo=(w_r - wr_hi.astype(jnp.float32)).astype(bf16),
            br=jnp.pad(b_r, (0, LANES - b_r.shape[0]))[None, :],
            wg=w_expert_gate[layer].astype(bf16), wu=w_expert_up[layer].astype(bf16),
            wd=w_expert_down[layer].astype(bf16))
        w16 = w_in[layer].astype(bf16)
        g = g_mix[layer][None, :]
        g_out = g_final[None, :]

        (q_sb, k_sb, v_sb, q_m, k_m, v_m, ga, gb,
         k_sb16, v_sb16, k_m16, v_m16, kmean) = _inproj(hp, g, w16, tab_p, tm_p, seq)
        o_sb = _sb_prompt(q_sb, k_sb16, v_sb16, batch, seq)
        o_m = _moba_prompt(q_m, k_m16, v_m16,
                           kmean.reshape(batch, seq // MOBA_BLOCK, WIDTH), batch, seq)
        hp = _tail(hp, o_sb, o_m, ga, gb, lw, g_out, 512, 1024)
        for dst, val in zip(outs[:4], (k_sb, v_sb, k_m, v_m)):
            dst.append(jnp.transpose(val.reshape(batch, N_HEADS, HEAD_DIM, seq), (0, 3, 1, 2)))

        (q_sb, k_sb, v_sb, q_m, k_m, v_m, ga, gb, *_) = _inproj(hs, g, w16, tab_s, tm_s)
        tok = lambda a: a.astype(jnp.float32).reshape(dec_b, dec_seq, WIDTH)
        pages = lambda cch: jnp.transpose(cch[layer], (0, 2, 3, 1)).reshape(
            n_phys, WIDTH, PAGE_SIZE)
        o_sb = _decode(page_table, tok(q_sb), tok(k_sb), tok(v_sb),
                       pages(cache_sb_k), pages(cache_sb_v), moba=False)
        o_m = _decode(page_table, tok(q_m), tok(k_m), tok(v_m),
                      pages(cache_moba_k), pages(cache_moba_v), moba=True)
        flat = lambda a: a.reshape(dec_b * dec_seq, WIDTH)
        hs = _tail(hs, flat(o_sb), flat(o_m), ga, gb, lw, g_out, tm_s * 2, tm_s * 2)
        for dst, val in zip(outs[4:], (k_sb, v_sb, k_m, v_m)):
            dst.append(val.reshape(dec_b, dec_seq, N_HEADS, HEAD_DIM))

    y_prompt = hp.reshape(batch, seq, D_MODEL)
    y_sample = hs.reshape(dec_b, dec_seq, D_MODEL)
    return (y_prompt, y_sample) + tuple(jnp.stack(o) for o in outs)
```

```python
import functools

import jax
import jax.numpy as jnp
from jax import lax
from jax.experimental import pallas as pl
from jax.experimental.pallas import tpu as pltpu

D_MODEL = 1024
HEAD_DIM = 64
N_HEADS = 8
WIDTH = N_HEADS * HEAD_DIM
IN_WIDTH = 6 * WIDTH + 2 * D_MODEL
ROPE_DIM = HEAD_DIM // 4
ROPE_HALF = ROPE_DIM // 2
ROPE_THETA = 500000.0
PAGE_SIZE = 128
MOBA_BLOCK = 256
MOBA_TOPK = 3
N_GROUPS = 4
EXPERTS_PER_GROUP = 4
N_EXPERTS = N_GROUPS * EXPERTS_PER_GROUP
D_EXPERT = 256
NORM_EPS = 1e-6
Q_SCALE = HEAD_DIM ** -0.5

LANES = 128
HEAD_PAIR = LANES // HEAD_DIM
N_PAIRS = N_HEADS // HEAD_PAIR
ATT_BLOCK = 128
Q_TILE = 256
MOE_ROW_CHUNK = 256
MOBA_PAIR_GROUP = 2
MERGE_ROW_CHUNK = 256
DECODE_ROWS = 2
VMEM_LIMIT = 56 * 1024 * 1024

MASKED = -1e30
EXP_UNDERFLOW = -104.0

_NT = (((1,), (1,)), ((), ()))


def _dot(a, b):
    return jnp.dot(a, b, preferred_element_type=jnp.float32)


def _dot_nt(a, b):
    return lax.dot_general(a, b, _NT, preferred_element_type=jnp.float32)


def _split2(x):
    hi = x.astype(jnp.bfloat16)
    lo = (x - hi.astype(jnp.float32)).astype(jnp.bfloat16)
    return hi, lo


def _log_sigmoid(z):
    return jnp.minimum(z, 0.0) - jnp.log(1.0 + jnp.exp(-jnp.abs(z)))


def _suffix_matrix():
    r = lax.broadcasted_iota(jnp.int32, (2 * ATT_BLOCK, 2 * ATT_BLOCK), 0) % ATT_BLOCK
    c = lax.broadcasted_iota(jnp.int32, (2 * ATT_BLOCK, 2 * ATT_BLOCK), 1)
    return jnp.where((c >= ATT_BLOCK) | (r > c), 1.0, 0.0).astype(jnp.bfloat16)


def _suffix_sums(lk, umat):
    return _dot(jnp.concatenate(_split2(lk), axis=1), umat)


def _rms_norm(x, g):
    r = lax.rsqrt(jnp.mean(x * x, axis=-1, keepdims=True) + NORM_EPS)
    return (x * r) * g


def _rope(h, ca, cb, cc):
    outs = []
    for g in range(WIDTH // LANES):
        xg = h[:, g * LANES:(g + 1) * LANES]
        outs.append(xg * ca + pltpu.roll(xg, LANES - ROPE_HALF, 1) * cb
                    + pltpu.roll(xg, ROPE_HALF, 1) * cc)
    return jnp.concatenate(outs, axis=1)


def _inproj_kernel(x_ref, g_ref, w_ref, ca_ref, cb_ref, cc_ref,
                   qsb_ref, ksb_ref, vsb_ref, qm_ref, km_ref, vm_ref, ga_ref, gb_ref,
                   ksb16_ref, vsb16_ref, km16_ref, vm16_ref, kmean_ref, *, transposed):
    xn = _rms_norm(x_ref[...], g_ref[...]).astype(jnp.bfloat16)
    ca, cb, cc = ca_ref[...], cb_ref[...], cc_ref[...]

    def proj(c, width=WIDTH):
        return _dot(xn, w_ref[:, c:c + width])

    def emit(ref, ref16, val):
        if transposed:
            ref[0] = val.T
        else:
            ref[...] = val
        ref16[...] = val.astype(jnp.bfloat16)

    qsb_ref[...] = (proj(0) * Q_SCALE).astype(jnp.bfloat16)
    emit(ksb_ref, ksb16_ref, proj(WIDTH))
    emit(vsb_ref, vsb16_ref, proj(2 * WIDTH))
    qm_ref[...] = _rope(proj(3 * WIDTH), ca, cb, cc)
    k = _rope(proj(4 * WIDTH), ca, cb, cc)
    emit(km_ref, km16_ref, k)
    for blk in range(kmean_ref.shape[0]):
        kmean_ref[blk] = jnp.sum(k[blk * MOBA_BLOCK:(blk + 1) * MOBA_BLOCK], axis=0,
                                 keepdims=True) * (1.0 / MOBA_BLOCK)
    emit(vm_ref, vm16_ref, proj(5 * WIDTH))
    for half in range(2):
        ga_ref[:, half * WIDTH:(half + 1) * WIDTH] = proj(6 * WIDTH + half * WIDTH)
        gb_ref[:, half * WIDTH:(half + 1) * WIDTH] = proj(6 * WIDTH + D_MODEL + half * WIDTH)


def _rope_tables(pos):
    inv = jnp.float32(ROPE_THETA) ** (-jnp.arange(ROPE_HALF, dtype=jnp.float32) / ROPE_HALF)
    ang = pos.astype(jnp.float32)[:, None] * inv[None, :]
    cos, sin = jnp.cos(ang), jnp.sin(ang)
    n = pos.shape[0]
    rest = HEAD_DIM - ROPE_DIM
    one = jnp.ones((n, rest), jnp.float32)
    zero = jnp.zeros((n, rest), jnp.float32)
    zh = jnp.zeros((n, ROPE_HALF), jnp.float32)
    a = jnp.concatenate([cos, cos, one], axis=1)
    b = jnp.concatenate([-sin, zh, zero], axis=1)
    c = jnp.concatenate([zh, sin, zero], axis=1)
    rep = LANES // HEAD_DIM
    return tuple(jnp.tile(t, (1, rep)) for t in (a, b, c))


def _inproj(x, g, w16, tables, tm, seq=None):
    t = x.shape[0]
    nt = tables[0].shape[0] // tm
    row = lambda i: (i, 0)
    fixed = lambda i: (0, 0)
    f32, bf16 = jnp.float32, jnp.bfloat16
    slab = lambda dt, w=WIDTH: jax.ShapeDtypeStruct((t, w), dt)
    if seq is None:
        kv_shape, kv_spec = slab(f32), pl.BlockSpec((tm, WIDTH), row)
    else:
        assert seq == nt * tm
        kv_shape = jax.ShapeDtypeStruct((t // seq, WIDTH, seq), f32)
        kv_spec = pl.BlockSpec((1, WIDTH, tm), lambda i: (i // nt, 0, i % nt))
    out_shape = (slab(bf16), kv_shape, kv_shape, slab(f32), kv_shape, kv_shape,
                 slab(f32, D_MODEL), slab(f32, D_MODEL),
                 slab(bf16), slab(bf16), slab(bf16), slab(bf16),
                 jax.ShapeDtypeStruct((t // MOBA_BLOCK, 1, WIDTH), f32))
    tok_spec = pl.BlockSpec((tm, WIDTH), row)
    out_specs = tuple([tok_spec, kv_spec, kv_spec, tok_spec, kv_spec, kv_spec]
                      + [pl.BlockSpec((tm, D_MODEL), row)] * 2
                      + [tok_spec] * 4
                      + [pl.BlockSpec((tm // MOBA_BLOCK, 1, WIDTH), lambda i: (i, 0, 0))])
    tab_spec = pl.BlockSpec((tm, LANES), lambda i: (i % nt, 0))
    return pl.pallas_call(
        functools.partial(_inproj_kernel, transposed=seq is not None),
        grid=(t // tm,),
        in_specs=[pl.BlockSpec((tm, D_MODEL), row),
                  pl.BlockSpec((1, D_MODEL), fixed),
                  pl.BlockSpec((D_MODEL, IN_WIDTH), fixed, pipeline_mode=pl.Buffered(1)),
                  tab_spec, tab_spec, tab_spec],
        out_specs=out_specs,
        out_shape=out_shape,
        compiler_params=pltpu.CompilerParams(
            dimension_semantics=("arbitrary",), vmem_limit_bytes=VMEM_LIMIT),
        name="inproj",
    )(x, g, w16, *tables)


def _stack_pair(x, zero):
    low = lax.broadcasted_iota(jnp.int32, x.shape, 1) < HEAD_DIM
    return jnp.concatenate([jnp.where(low, x, zero), jnp.where(low, zero, x)], axis=0)


def _unstack_pair(y):
    half = y.shape[0] // 2
    low = lax.broadcasted_iota(jnp.int32, (half, LANES), 1) < HEAD_DIM
    return jnp.where(low, y[:half], y[half:])


def _sb_prompt_kernel(q_ref, k_ref, v_ref, o_ref, q2_ref, run_ref, acc_ref):
    qi = pl.program_id(1)
    umat = _suffix_matrix()
    pairs = range(N_PAIRS)
    lanes = [slice(p * LANES, (p + 1) * LANES) for p in pairs]
    for p in pairs:
        qp = q_ref[:, lanes[p]]
        q2_ref[p] = _stack_pair(qp, jnp.zeros_like(qp))

    def tile(kb, first, masked, skip=0):
        assert first or not skip
        start = pl.multiple_of(kb * ATT_BLOCK, ATT_BLOCK)
        live = Q_TILE - skip
        if skip:
            q2 = [jnp.concatenate([q2_ref[p, skip:Q_TILE, :], q2_ref[p, Q_TILE + skip:, :]],
                                  axis=0) for p in pairs]
        else:
            q2 = [q2_ref[p] for p in pairs]
        if masked:
            r = lax.broadcasted_iota(jnp.int32, (HEAD_PAIR * live, ATT_BLOCK), 0) % live + skip
            c = lax.broadcasted_iota(jnp.int32, (HEAD_PAIR * live, ATT_BLOCK), 1)
            causal = kb * ATT_BLOCK + c < qi * Q_TILE + r
        z = [_dot_nt(q2[p], k_ref[pl.ds(start, ATT_BLOCK), lanes[p]]) for p in pairs]
        ls = [_log_sigmoid(z[p]) for p in pairs]
        lk = [ls[p] - z[p] for p in pairs]
        if masked:
            lk = [jnp.where(causal, lk[p], 0.0) for p in pairs]
        s = [_suffix_sums(lk[p], umat) for p in pairs]
        e = [ls[p] + s[p][:, :ATT_BLOCK] for p in pairs]
        if not first:
            e = [e[p] + run_ref[p] for p in pairs]
        a = [jnp.exp(e[p]) for p in pairs]
        if masked:
            a = [jnp.where(causal, a[p], 0.0) for p in pairs]
        pv = [_unstack_pair(_dot(a[p].astype(jnp.bfloat16),
                                 v_ref[pl.ds(start, ATT_BLOCK), lanes[p]])) for p in pairs]
        run = [s[p][:, ATT_BLOCK:] if first else run_ref[p] + s[p][:, ATT_BLOCK:] for p in pairs]
        if skip:
            zeros = jnp.zeros((skip, LANES), jnp.float32)
            run = [jnp.concatenate([zeros, run[p][:live], zeros, run[p][live:]], axis=0)
                   for p in pairs]
            pv = [jnp.concatenate([zeros, pv[p]], axis=0) for p in pairs]
        for p in pairs:
            run_ref[p] = run[p]
            if first:
                acc_ref[p] = pv[p]
            else:
                acc_ref[p] += pv[p]
        return jnp.max(functools.reduce(jnp.maximum, run))

    n_diag = Q_TILE // ATT_BLOCK
    last = (qi + 1) * n_diag - 1
    reach = tile(last, True, True, skip=(n_diag - 1) * ATT_BLOCK)
    for j in range(1, n_diag):
        reach = tile(last - j, False, True)

    def more(carry):
        j, reach = carry
        return jnp.logical_and(j <= last, reach > EXP_UNDERFLOW)

    def body(carry):
        j, _ = carry
        return j + 1, tile(last - j, False, False)

    lax.while_loop(more, body, (n_diag, reach))
    for p in pairs:
        o_ref[:, lanes[p]] = acc_ref[p].astype(o_ref.dtype)


def _sb_prompt(q16, k16, v16, batch, seq):
    nq = seq // Q_TILE
    qspec = pl.BlockSpec((Q_TILE, WIDTH), lambda b, i: (b * nq + i, 0))
    kvspec = pl.BlockSpec((seq, WIDTH), lambda b, i: (b, 0))
    stacked = (N_PAIRS, HEAD_PAIR * Q_TILE, LANES)
    return pl.pallas_call(
        _sb_prompt_kernel,
        grid=(batch, nq),
        in_specs=[qspec, kvspec, kvspec],
        out_specs=qspec,
        out_shape=jax.ShapeDtypeStruct((batch * seq, WIDTH), jnp.bfloat16),
        scratch_shapes=[pltpu.VMEM(stacked, jnp.bfloat16),
                        pltpu.VMEM(stacked, jnp.float32),
                        pltpu.VMEM((N_PAIRS, Q_TILE, LANES), jnp.float32)],
        compiler_params=pltpu.CompilerParams(
            dimension_semantics=("arbitrary", "arbitrary"), vmem_limit_bytes=VMEM_LIMIT),
        name="sb_prompt",
    )(q16, k16, v16)


def _block_scores(qf, kmean_rows):
    pad = jnp.zeros((LANES - kmean_rows.shape[0], kmean_rows.shape[1]), jnp.float32)
    km = jnp.concatenate([kmean_rows, pad], axis=0)
    qh, ql = _split2(qf)
    kh, kl = _split2(km)
    return _dot_nt(qh, kh) + _dot_nt(qh, kl) + _dot_nt(ql, kh)


def _select_bias(s, n_past, n_blocks):
    lane = lax.broadcasted_iota(jnp.int32, s.shape, 1)
    rank = jnp.zeros(s.shape, jnp.float32)
    for m in range(n_blocks):
        sm = s[:, m:m + 1]
        beats = (sm > s) | ((sm == s) & (m < lane))
        rank = rank + jnp.where(beats, 1.0, 0.0) * jnp.where(m < n_past, 1.0, 0.0)
    sel = (lane < n_past) & (rank < float(MOBA_TOPK))
    return jnp.where(sel, 0.0, -jnp.inf)


def _select_bias_rows(s, n_past):
    n_blocks = s.shape[0]
    n = lax.broadcasted_iota(jnp.int32, s.shape, 0)
    rank = jnp.zeros(s.shape, jnp.float32)
    for d in range(1, n_blocks):
        wrap = n + d >= n_blocks
        m = jnp.where(wrap, n + d - n_blocks, n + d)
        sm = pltpu.roll(s, n_blocks - d, 0)
        beats = (sm > s) | ((sm == s) & wrap)
        rank = rank + jnp.where(beats & (m < n_past), 1.0, 0.0)
    sel = (n < n_past) & (rank < float(MOBA_TOPK))
    return jnp.where(sel, 0.0, MASKED)


def _moba_prompt_kernel(q_ref, k_ref, v_ref, kmean_ref, o_ref,
                        q2_ref, neg_ref, m_ref, l_ref, acc_ref):
    qi = pl.program_id(1)
    per_block = MOBA_BLOCK // Q_TILE
    own = qi // per_block
    rows = HEAD_PAIR * Q_TILE
    n_blocks = kmean_ref.shape[1]
    pairs = range(N_PAIRS)
    lanes = [slice(p * LANES, (p + 1) * LANES) for p in pairs]
    r = lax.broadcasted_iota(jnp.int32, (rows, MOBA_BLOCK), 0)
    c = lax.broadcasted_iota(jnp.int32, (rows, MOBA_BLOCK), 1)
    own_mask = c <= r % Q_TILE + (qi % per_block) * Q_TILE
    kmean = kmean_ref[0]

    neg_t = []
    for p in pairs:
        qf = q_ref[:, lanes[p]]
        q2 = _stack_pair(qf, jnp.zeros_like(qf))
        q2_ref[p] = (q2 * Q_SCALE).astype(jnp.bfloat16)
        qh, ql = _split2(q2)
        kh, kl = _split2(kmean[:, lanes[p]])
        scores_t = _dot_nt(kh, qh) + _dot_nt(kl, qh) + _dot_nt(kh, ql)
        neg_t.append(_select_bias_rows(scores_t, own))
    neg_t.append(jnp.zeros((LANES - N_PAIRS * n_blocks, rows), jnp.float32))
    neg_ref[...] = jnp.concatenate(neg_t, axis=0).T.astype(jnp.bfloat16)
    ones = jnp.ones((MOBA_BLOCK, LANES), jnp.bfloat16)
    sub16 = lax.broadcasted_iota(jnp.int32, (16, LANES), 1)

    def block(n, is_own):
        for g in range(0, N_PAIRS, MOBA_PAIR_GROUP):
            stages(n, is_own, range(g, g + MOBA_PAIR_GROUP))

    def stages(n, is_own, pairs):
        start = pl.multiple_of(n * MOBA_BLOCK, MOBA_BLOCK)
        kp = {p: k_ref[pl.ds(start, MOBA_BLOCK), lanes[p]] for p in pairs}
        vp = {p: jnp.concatenate([v_ref[pl.ds(start, MOBA_BLOCK), lanes[p]], ones], axis=1)
              for p in pairs}
        if is_own:
            logits = {p: jnp.where(own_mask, _dot_nt(q2_ref[p], kp[p]), -jnp.inf) for p in pairs}
            m_new = {p: jnp.broadcast_to(jnp.max(logits[p], axis=1, keepdims=True), (rows, LANES))
                     for p in pairs}
        else:
            neg = neg_ref[...]
            pick = {p: jnp.concatenate(
                [jnp.where(sub16 == p * n_blocks + n, 1.0, 0.0).astype(jnp.bfloat16)]
                * (MOBA_BLOCK // 16), axis=0) for p in pairs}
            logits = {p: _dot_nt(jnp.concatenate([q2_ref[p], neg], axis=1),
                                 jnp.concatenate([kp[p], pick[p]], axis=1)) for p in pairs}
            m_old = {p: m_ref[p] for p in pairs}
            m_new = {p: jnp.maximum(m_old[p], jnp.max(logits[p], axis=1, keepdims=True))
                     for p in pairs}
        pr = {p: jnp.exp(logits[p] - jnp.concatenate([m_new[p]] * (MOBA_BLOCK // LANES), axis=1))
              for p in pairs}
        pv = {p: _dot(pr[p].astype(jnp.bfloat16), vp[p]) for p in pairs}
        for p in pairs:
            if is_own:
                l_ref[p] = pv[p][:, LANES:]
                acc_ref[p] = pv[p][:, :LANES]
            else:
                alpha = jnp.exp(m_old[p] - m_new[p])
                l_ref[p] = alpha * l_ref[p] + pv[p][:, LANES:]
                acc_ref[p] = alpha * acc_ref[p] + pv[p][:, :LANES]
            m_ref[p] = m_new[p]

    block(own, True)

    def body(n, carry):
        block(n, False)
        return carry

    lax.fori_loop(0, own, body, 0)
    for p in range(N_PAIRS):
        o_ref[:, p * LANES:(p + 1) * LANES] = _unstack_pair(
            acc_ref[p] / l_ref[p]).astype(o_ref.dtype)


def _moba_prompt(qf, k16, v16, kmean, batch, seq):
    nq = seq // Q_TILE
    nb = seq // MOBA_BLOCK
    qspec = pl.BlockSpec((Q_TILE, WIDTH), lambda b, i: (b * nq + i, 0))
    kvspec = pl.BlockSpec((seq, WIDTH), lambda b, i: (b, 0))
    stacked = (N_PAIRS, HEAD_PAIR * Q_TILE, LANES)
    return pl.pallas_call(
        _moba_prompt_kernel,
        grid=(batch, nq),
        in_specs=[qspec, kvspec, kvspec,
                  pl.BlockSpec((1, nb, WIDTH), lambda b, i: (b, 0, 0))],
        out_specs=qspec,
        out_shape=jax.ShapeDtypeStruct((batch * seq, WIDTH), jnp.bfloat16),
        scratch_shapes=[pltpu.VMEM(stacked, jnp.bfloat16),
                        pltpu.VMEM(stacked[1:], jnp.bfloat16)]
        + [pltpu.VMEM(stacked, jnp.float32)] * 3,
        compiler_params=pltpu.CompilerParams(
            dimension_semantics=("arbitrary", "arbitrary"), vmem_limit_bytes=VMEM_LIMIT),
        name="moba_prompt",
    )(qf, k16, v16, kmean)


def _decode_kernel(pt_ref, q_ref, knew_ref, vnew_ref, *refs, moba, n_pages, n_new, n_rows):
    del pt_ref
    k_pages = [refs[i * n_pages:(i + 1) * n_pages] for i in range(n_rows)]
    v_pages = [refs[(n_rows + i) * n_pages:(n_rows + i + 1) * n_pages] for i in range(n_rows)]
    o_ref = refs[2 * n_rows * n_pages]
    rows = n_new * N_HEADS
    sub = lax.broadcasted_iota(jnp.int32, (N_HEADS, WIDTH), 0)
    ln = lax.broadcasted_iota(jnp.int32, (N_HEADS, WIDTH), 1)
    head_mask = (ln // HEAD_DIM) == sub
    r = lax.broadcasted_iota(jnp.int32, (rows, PAGE_SIZE), 0)
    c = lax.broadcasted_iota(jnp.int32, (rows, PAGE_SIZE), 1)
    zpad = jnp.zeros((PAGE_SIZE - n_new, WIDTH), jnp.float32)
    page16 = lambda ref: ref[0].astype(jnp.bfloat16)
    order = list(reversed(range(n_pages)))
    samples = range(n_rows)

    qbd, k_new, v_new = [], [], []
    for i in samples:
        q = q_ref[i]
        qbd.append(jnp.concatenate(
            [jnp.where(head_mask, jnp.broadcast_to(q[j:j + 1, :], (N_HEADS, WIDTH)), 0.0)
             for j in range(n_new)], axis=0))
        k_new.append(jnp.concatenate([knew_ref[i], zpad], axis=0).astype(jnp.bfloat16))
        v_new.append(jnp.concatenate([vnew_ref[i], zpad], axis=0).astype(jnp.bfloat16))

    def weighted_values(i, w):
        acc = _dot(w[0].astype(jnp.bfloat16), v_new[i])
        for j, p in enumerate(order):
            acc = acc + _dot_nt(w[j + 1].astype(jnp.bfloat16), page16(v_pages[i][p]))
        return acc

    if moba:
        per_block = MOBA_BLOCK // PAGE_SIZE
        n_blocks = n_pages // per_block
        lane = lax.broadcasted_iota(jnp.int32, (WIDTH, LANES), 1)
        qb = [(qbd[i] * Q_SCALE).astype(jnp.bfloat16) for i in samples]
        bias = []
        for i in samples:
            kmean_t = jnp.zeros((WIDTH, LANES), jnp.float32)
            for n in range(n_blocks):
                block_sum = sum(k_pages[i][n * per_block + j][0] for j in range(per_block))
                col = jnp.sum(block_sum, axis=1, keepdims=True) * (1.0 / MOBA_BLOCK)
                kmean_t = jnp.where(lane == n, col, kmean_t)
            qh, ql = _split2(qbd[i])
            kh, kl = _split2(kmean_t)
            bias.append(_select_bias(_dot(qh, kh) + _dot(qh, kl) + _dot(ql, kh),
                                     n_blocks, n_blocks))
        logits = [[jnp.where(c <= r // N_HEADS, _dot_nt(qb[i], k_new[i]), -jnp.inf)]
                  + [_dot(qb[i], page16(k_pages[i][p]))
                     + bias[i][:, p // per_block:p // per_block + 1] for p in order]
                  for i in samples]
        m = [jnp.max(functools.reduce(jnp.maximum, logits[i]), axis=1, keepdims=True)
             for i in samples]
        pr = [[jnp.exp(lg - m[i]) for lg in logits[i]] for i in samples]
        l = [jnp.sum(functools.reduce(jnp.add, pr[i]), axis=1, keepdims=True) for i in samples]
        acc = [weighted_values(i, pr[i]) / l[i] for i in samples]
    else:
        umat = _suffix_matrix()
        new_mask = c < r // N_HEADS
        qb = [qbd[i].astype(jnp.bfloat16) for i in samples]
        z = [[_dot_nt(qb[i], k_new[i])] + [_dot(qb[i], page16(k_pages[i][p])) for p in order]
             for i in samples]
        n_chunks = n_pages + 1
        ls = [[_log_sigmoid(zc) for zc in z[i]] for i in samples]
        lk = [[ls[i][j] - z[i][j] for j in range(n_chunks)] for i in samples]
        for i in samples:
            lk[i][0] = jnp.where(new_mask, lk[i][0], 0.0)
        s = _suffix_sums(jnp.concatenate([x for i in samples for x in lk[i]], axis=0), umat)
        acc = []
        for i in samples:
            a, run = [], None
            for j in range(n_chunks):
                sj = s[(i * n_chunks + j) * rows:(i * n_chunks + j + 1) * rows]
                e = ls[i][j] + sj[:, :PAGE_SIZE]
                a.append(jnp.exp(e if run is None else e + run))
                run = sj[:, PAGE_SIZE:] if run is None else run + sj[:, PAGE_SIZE:]
            a[0] = jnp.where(new_mask, a[0], 0.0)
            acc.append(weighted_values(i, a))

    for i in samples:
        o_ref[i] = jnp.concatenate(
            [jnp.sum(jnp.where(head_mask, acc[i][j * N_HEADS:(j + 1) * N_HEADS, :], 0.0),
                     axis=0, keepdims=True) for j in range(n_new)], axis=0)


def _decode(page_table, q, k_new, v_new, cache_k, cache_v, moba):
    b, n_new, _ = q.shape
    n_pages = page_table.shape[1]
    n_rows = DECODE_ROWS if b % DECODE_ROWS == 0 else 1
    tok_spec = pl.BlockSpec((n_rows, n_new, WIDTH), lambda i, pt: (i, 0, 0))

    def page_spec(row, p):
        return pl.BlockSpec((1, WIDTH, PAGE_SIZE),
                            lambda i, pt: (pt[i * n_rows + row, p], 0, 0))

    page_specs = [page_spec(row, p) for row in range(n_rows) for p in range(n_pages)]
    grid_spec = pltpu.PrefetchScalarGridSpec(
        num_scalar_prefetch=1,
        grid=(b // n_rows,),
        in_specs=[tok_spec, tok_spec, tok_spec] + page_specs * 2,
        out_specs=tok_spec,
    )
    n_ops = n_rows * n_pages
    return pl.pallas_call(
        functools.partial(_decode_kernel, moba=moba, n_pages=n_pages, n_new=n_new,
                          n_rows=n_rows),
        grid_spec=grid_spec,
        out_shape=jax.ShapeDtypeStruct((b, n_new, WIDTH), jnp.float32),
        compiler_params=pltpu.CompilerParams(
            dimension_semantics=("arbitrary",), vmem_limit_bytes=VMEM_LIMIT),
        name="moba_decode" if moba else "sb_decode",
    )(page_table, q, k_new, v_new, *([cache_k] * n_ops), *([cache_v] * n_ops))


def _route(lg):
    lane = lax.broadcasted_iota(jnp.int32, lg.shape, 1)
    lane_f = lane.astype(jnp.float32)
    far = float(LANES)
    is_group = lane < N_GROUPS
    gl = jnp.where(is_group, lg, -jnp.inf)
    gmax = jnp.max(gl, axis=1, keepdims=True)
    gsum = jnp.sum(jnp.where(is_group, jnp.exp(gl - gmax), 0.0), axis=1, keepdims=True)
    g_p = 1.0 / gsum
    g_i = jnp.min(jnp.where(gl == gmax, lane_f, far), axis=1, keepdims=True)
    lo = N_GROUPS + EXPERTS_PER_GROUP * g_i
    in_group = (lane_f >= lo) & (lane_f < lo + EXPERTS_PER_GROUP)
    el = jnp.where(in_group, lg, -jnp.inf)
    m1 = jnp.max(el, axis=1, keepdims=True)
    i1 = jnp.min(jnp.where(el == m1, lane_f, far), axis=1, keepdims=True)
    el2 = jnp.where(lane_f == i1, -jnp.inf, el)
    m2 = jnp.max(el2, axis=1, keepdims=True)
    i2 = jnp.min(jnp.where(el2 == m2, lane_f, far), axis=1, keepdims=True)
    t = jnp.exp(m2 - m1)
    w1 = g_p / (1.0 + t)
    w2 = w1 * t
    return jnp.where(lane_f == i1, w1, 0.0) + jnp.where(lane_f == i2, w2, 0.0)


def _merge_kernel(x_ref, osb_ref, om_ref, ga_ref, gb_ref, wps_ref, wpm_ref, wout_ref,
                  g_ref, wrh_ref, wrl_ref, br_ref, hp_ref, xn_ref, comb_ref):
    tm = x_ref.shape[0]
    rc = min(tm, MERGE_ROW_CHUNK)
    chunks = [slice(c * rc, (c + 1) * rc) for c in range(tm // rc)]

    def branches(rows):
        return (_dot(osb_ref[rows, :].astype(jnp.bfloat16), wps_ref[...]),
                _dot(om_ref[rows, :].astype(jnp.bfloat16), wpm_ref[...]))

    def residual(rows, b):
        m = jax.nn.sigmoid(ga_ref[rows, :]) * b[0] + jax.nn.sigmoid(gb_ref[rows, :]) * b[1]
        hp = x_ref[rows, :] + _dot(m.astype(jnp.bfloat16), wout_ref[...])
        hp_ref[rows, :] = hp
        return hp

    def route(rows, hp):
        xn = _rms_norm(hp, g_ref[...])
        xn_ref[rows, :] = xn.astype(jnp.bfloat16)
        xh, xl = _split2(xn)
        wh, wl = wrh_ref[...], wrl_ref[...]
        lg = _dot(xh, wh) + _dot(xh, wl) + _dot(xl, wh) + br_ref[...]
        comb_ref[rows, :] = _route(lg)

    b = [branches(rows) for rows in chunks]
    hp = [residual(rows, b[i]) for i, rows in enumerate(chunks)]
    for i, rows in enumerate(chunks):
        route(rows, hp[i])


def _merge(x, o_sb, o_m, ga, gb, wps, wpm, wout, g_ffn, wr_hi, wr_lo, br, tm):
    t = x.shape[0]
    row = lambda i: (i, 0)
    fixed = lambda i: (0, 0)
    return pl.pallas_call(
        _merge_kernel,
        grid=(t // tm,),
        in_specs=[pl.BlockSpec((tm, D_MODEL), row),
                  pl.BlockSpec((tm, WIDTH), row),
                  pl.BlockSpec((tm, WIDTH), row),
                  pl.BlockSpec((tm, D_MODEL), row),
                  pl.BlockSpec((tm, D_MODEL), row),
                  pl.BlockSpec((WIDTH, D_MODEL), fixed),
                  pl.BlockSpec((WIDTH, D_MODEL), fixed),
                  pl.BlockSpec((D_MODEL, D_MODEL), fixed),
                  pl.BlockSpec((1, D_MODEL), fixed),
                  pl.BlockSpec((D_MODEL, LANES), fixed),
                  pl.BlockSpec((D_MODEL, LANES), fixed),
                  pl.BlockSpec((1, LANES), fixed)],
        out_specs=(pl.BlockSpec((tm, D_MODEL), row),
                   pl.BlockSpec((tm, D_MODEL), row),
                   pl.BlockSpec((tm, LANES), row)),
        out_shape=(jax.ShapeDtypeStruct((t, D_MODEL), jnp.float32),
                   jax.ShapeDtypeStruct((t, D_MODEL), jnp.bfloat16),
                   jax.ShapeDtypeStruct((t, LANES), jnp.float32)),
        compiler_params=pltpu.CompilerParams(
            dimension_semantics=("arbitrary",), vmem_limit_bytes=VMEM_LIMIT),
        name="merge",
    )(x, o_sb, o_m, ga, gb, wps, wpm, wout, g_ffn, wr_hi, wr_lo, br)


def _moe_kernel(xn_ref, comb_ref, hp_ref, wg_ref, wu_ref, wd_ref, gf_ref, y_ref):
    tm = xn_ref.shape[0]
    rc = min(tm, MOE_ROW_CHUNK)
    n_chunks = tm // rc
    y_ref[...] = hp_ref[...]
    lane = lax.broadcasted_iota(jnp.int32, (rc, LANES), 1)

    def gate_up(e, c):
        xc = xn_ref[c * rc:(c + 1) * rc, :]
        return _dot(xc, wg_ref[e]), _dot(xc, wu_ref[e])

    def expert(e, nxt):
        for c in range(n_chunks):
            a, u = nxt
            if c + 1 < n_chunks:
                nxt = gate_up(e, c + 1)
            else:
                nxt = gate_up(jnp.minimum(e + 1, N_EXPERTS - 1), 0)
            rows = slice(c * rc, (c + 1) * rc)
            w = jnp.sum(jnp.where(lane == e + N_GROUPS, comb_ref[rows, :], 0.0),
                        axis=1, keepdims=True)
            hdn = (a * jax.nn.sigmoid(a)) * u * w
            y_ref[rows, :] += _dot(hdn.astype(jnp.bfloat16), wd_ref[e])
        return nxt

    lax.fori_loop(0, N_EXPERTS, expert, gate_up(0, 0))
    y_ref[...] = _rms_norm(y_ref[...], gf_ref[...])


def _moe(xn16, comb, hp, wg, wu, wd, g_final, tm):
    t = xn16.shape[0]
    row = lambda i: (i, 0)
    resident = lambda shape: pl.BlockSpec(shape, lambda i: (0,) * len(shape),
                                          pipeline_mode=pl.Buffered(1))
    return pl.pallas_call(
        _moe_kernel,
        grid=(t // tm,),
        in_specs=[pl.BlockSpec((tm, D_MODEL), row),
                  pl.BlockSpec((tm, LANES), row),
                  pl.BlockSpec((tm, D_MODEL), row),
                  resident((N_EXPERTS, D_MODEL, D_EXPERT)),
                  resident((N_EXPERTS, D_MODEL, D_EXPERT)),
                  resident((N_EXPERTS, D_EXPERT, D_MODEL)),
                  resident((1, D_MODEL))],
        out_specs=pl.BlockSpec((tm, D_MODEL), row),
        out_shape=jax.ShapeDtypeStruct((t, D_MODEL), jnp.float32),
        compiler_params=pltpu.CompilerParams(
            dimension_semantics=("arbitrary",), vmem_limit_bytes=VMEM_LIMIT),
        name="moe",
    )(xn16, comb, hp, wg, wu, wd, g_final)


def _tail(x, o_sb, o_m, ga, gb, lw, g_out, tm_merge, tm_moe):
    hp, xn16, comb = _merge(x, o_sb, o_m, ga, gb, lw["wps"], lw["wpm"], lw["wout"],
                            lw["g_ffn"], lw["wr_hi"], lw["wr_lo"], lw["br"], tm_merge)
    return _moe(xn16, comb, hp, lw["wg"], lw["wu"], lw["wd"], g_out, tm_moe)


def kernel(x_prompt, x_sample, cache_sb_k, cache_sb_v, cache_moba_k, cache_moba_v, page_table,
           g_mix, w_in, w_proj_sb, w_proj_moba, w_out, g_ffn, w_router_group, b_router_group,
           w_router_expert, b_router_expert, w_expert_gate, w_expert_up, w_expert_down, g_final):
    batch, seq, _ = x_prompt.shape
    dec_b, dec_seq, _ = x_sample.shape
    depth = w_in.shape[0]
    assert depth == 1, "the final RMSNorm is fused into the last layer's MoE kernel"
    n_phys = cache_sb_k.shape[1]
    past_len = page_table.shape[1] * PAGE_SIZE
    bf16 = jnp.bfloat16
    tm_p = 512
    tm_s = min(256, dec_b * dec_seq)

    tab_p = _rope_tables(jnp.arange(tm_p * (seq // tm_p)))
    tab_s = _rope_tables(past_len + (jnp.arange(tm_s) % dec_seq))

    hp = x_prompt.reshape(batch * seq, D_MODEL)
    hs = x_sample.reshape(dec_b * dec_seq, D_MODEL)
    outs = [[] for _ in range(8)]
    for layer in range(depth):
        w_r = jnp.concatenate([w_router_group[layer], w_router_expert[layer]], axis=1)
        w_r = jnp.pad(w_r, ((0, 0), (0, LANES - w_r.shape[1])))
        wr_hi = w_r.astype(bf16)
        b_r = jnp.concatenate([b_router_group[layer], b_router_expert[layer]])
        lw = dict(
            wps=w_proj_sb[layer].astype(bf16), wpm=w_proj_moba[layer].astype(bf16),
            wout=w_out[layer].astype(bf16), g_ffn=g_ffn[layer][None, :],
            wr_hi=wr_hi, wr_lo=(w_r - wr_hi.astype(jnp.float32)).astype(bf16),
            br=jnp.pad(b_r, (0, LANES - b_r.shape[0]))[None, :],
            wg=w_expert_gate[layer].astype(bf16), wu=w_expert_up[layer].astype(bf16),
            wd=w_expert_down[layer].astype(bf16))
        w16 = w_in[layer].astype(bf16)
        g = g_mix[layer][None, :]
        g_out = g_final[None, :]

        (q_sb, k_sb, v_sb, q_m, k_m, v_m, ga, gb,
         k_sb16, v_sb16, k_m16, v_m16, kmean) = _inproj(hp, g, w16, tab_p, tm_p, seq)
        o_sb = _sb_prompt(q_sb, k_sb16, v_sb16, batch, seq)
        o_m = _moba_prompt(q_m, k_m16, v_m16,
                           kmean.reshape(batch, seq // MOBA_BLOCK, WIDTH), batch, seq)
        hp = _tail(hp, o_sb, o_m, ga, gb, lw, g_out, 1024, 1024)
        for dst, val in zip(outs[:4], (k_sb, v_sb, k_m, v_m)):
            dst.append(jnp.transpose(val.reshape(batch, N_HEADS, HEAD_DIM, seq), (0, 3, 1, 2)))

        (q_sb, k_sb, v_sb, q_m, k_m, v_m, ga, gb, *_) = _inproj(hs, g, w16, tab_s, tm_s)
        tok = lambda a: a.astype(jnp.float32).reshape(dec_b, dec_seq, WIDTH)
        pages = lambda cch: jnp.transpose(cch[layer], (0, 2, 3, 1)).reshape(
            n_phys, WIDTH, PAGE_SIZE)
        o_sb = _decode(page_table, tok(q_sb), tok(k_sb), tok(v_sb),
                       pages(cache_sb_k), pages(cache_sb_v), moba=False)
        o_m = _decode(page_table, tok(q_m), tok(k_m), tok(v_m),
                      pages(cache_moba_k), pages(cache_moba_v), moba=True)
        flat = lambda a: a.reshape(dec_b * dec_seq, WIDTH)
        hs = _tail(hs, flat(o_sb), flat(o_m), ga, gb, lw, g_out, tm_s * 2, tm_s * 2)
        for dst, val in zip(outs[4:], (k_sb, v_sb, k_m, v_m)):
            dst.append(val.reshape(dec_b, dec_seq, N_HEADS, HEAD_DIM))

    y_prompt = hp.reshape(batch, seq, D_MODEL)
    y_sample = hs.reshape(dec_b, dec_seq, D_MODEL)
    return (y_prompt, y_sample) + tuple(jnp.stack(o) for o in outs)
```
